```python
import jax
import jax.numpy as jnp
from jax import lax
import numpy as np

D_MODEL = 4096
BATCH = 4
SEQ = 4096
DEPTH = 2

GRID_W = 64
EPS = 1e-6
A_HEADS = 8
A_WIDTH = D_MODEL
A_V_DIM = A_WIDTH // A_HEADS
A_QK_DIM = A_V_DIM // 2
A_QK_WIDTH = A_HEADS * A_QK_DIM
A_CONV_W = 3
MLSTM_CHUNK = 128
B_WIDTH = D_MODEL
B_HEAD_DIM = 128
B_HEADS = B_WIDTH // B_HEAD_DIM
NA_WIN_R = 8
NA_WIN_C = 16
C_WIDTH = 2 * D_MODEL
C_GROUPS = 8
C_CHUNK = 128
LAYER0_SPLITS = (A_QK_WIDTH, A_QK_WIDTH, A_WIDTH, A_WIDTH, A_WIDTH, 4 * A_HEADS, B_WIDTH, B_WIDTH, B_WIDTH, B_WIDTH)
LAYER0_COLS = 2 * A_QK_WIDTH + 3 * A_WIDTH + 4 * A_HEADS + 4 * B_WIDTH

kernel_name = 'hybrid_mlstm_natten_gmlp_encoder'


def rms_norm(x, g):
    xf = x.astype(jnp.float32)
    y = xf * lax.rsqrt(jnp.mean(xf * xf, axis=-1, keepdims=True) + EPS)
    return (y * g.astype(jnp.float32)).astype(x.dtype)


def modulation(c, w_ada, b_ada):
    mod = jax.nn.silu(c) @ w_ada + b_ada
    shift, scale, gate = jnp.split(mod, 3, axis=-1)
    return shift[:, None, :], scale[:, None, :], gate[:, None, :]


def centred_dwconv(x, w):
    taps = w.shape[0]
    pad = taps // 2
    seq = x.shape[1]
    xp = jnp.pad(x, ((0, 0), (pad, pad), (0, 0)))
    y = xp[:, 0:seq] * w[0]
    for j in range(1, taps):
        y = y + xp[:, j:j + seq] * w[j]
    return y


def mlstm_one_direction(q, k, v, ig, lf):
    bsz, nh, seq, dk = q.shape
    dv = v.shape[-1]
    L = MLSTM_CHUNK
    nc = seq // L

    def to_chunks(a):
        a = a.reshape(bsz, nh, nc, L, *a.shape[3:])
        return jnp.moveaxis(a, 2, 0)

    xs = (to_chunks(q), to_chunks(k), to_chunks(v), to_chunks(ig), to_chunks(lf))
    tril = jnp.tril(jnp.ones((L, L), dtype=bool))

    def step(carry, chunk):
        C, n, m = carry
        qc, kc, vc, igc, lfc = chunk
        g = jnp.cumsum(lfc, axis=-1)
        D = g[..., :, None] - g[..., None, :] + igc[..., None, :]
        D = jnp.where(tril, D, -jnp.inf)
        m_t = jnp.maximum(g + m[..., None], jnp.max(D, axis=-1))
        P = jnp.exp(D - m_t[..., None])
        inter = jnp.exp(g + m[..., None] - m_t)
        S_ = jnp.einsum('bhtd,bhsd->bhts', qc, kc) * P
        num = jnp.einsum('bhts,bhsv->bhtv', S_, vc) + inter[..., None] * jnp.einsum('bhtd,bhdv->bhtv', qc, C)
        den = jnp.sum(S_, axis=-1) + inter * jnp.einsum('bhtd,bhd->bht', qc, n)
        h = num / jnp.maximum(jnp.abs(den), jnp.exp(-m_t))[..., None]
        gL = g[..., -1]
        ds = gL[..., None] - g + igc
        m_new = jnp.maximum(gL + m, jnp.max(ds, axis=-1))
        w = jnp.exp(ds - m_new[..., None])
        decay = jnp.exp(gL + m - m_new)
        C_new = decay[..., None, None] * C + jnp.einsum('bhs,bhsd,bhsv->bhdv', w, kc, vc)
        n_new = decay[..., None] * n + jnp.einsum('bhs,bhsd->bhd', w, kc)
        return (C_new, n_new, m_new), h

    init = (jnp.zeros((bsz, nh, dk, dv), jnp.float32),
            jnp.zeros((bsz, nh, dk), jnp.float32),
            jnp.zeros((bsz, nh), jnp.float32))
    _, h = lax.scan(step, init, xs)
    return jnp.moveaxis(h, 0, 2).reshape(bsz, nh, seq, dv)


def mlstm_mixer(a_q, a_k, a_v, a_o, a_gates, a_conv_w, a_gate_b, a_norm_g):
    bsz, seq, _ = a_v.shape
    qk = jax.nn.silu(centred_dwconv(jnp.concatenate([a_q, a_k], axis=-1), a_conv_w))

    def heads(t, d):
        return t.reshape(bsz, seq, A_HEADS, d).transpose(0, 2, 1, 3).astype(jnp.float32)

    q = heads(qk[..., :A_QK_WIDTH], A_QK_DIM) * (A_QK_DIM ** -0.5)
    k = heads(qk[..., A_QK_WIDTH:], A_QK_DIM)
    v = heads(a_v, A_V_DIM)
    g = (a_gates.astype(jnp.float32) + a_gate_b.astype(jnp.float32)).reshape(bsz, seq, 4, A_HEADS).transpose(2, 0, 3, 1)
    h_f = mlstm_one_direction(q, k, v, g[0], jax.nn.log_sigmoid(g[1]))
    fl = lambda t: jnp.flip(t, axis=2)
    h_b = fl(mlstm_one_direction(fl(q), fl(k), fl(v), fl(g[2]), fl(jax.nn.log_sigmoid(g[3]))))
    h = (h_f + h_b).transpose(0, 2, 1, 3)
    h = rms_norm(h, a_norm_g).reshape(bsz, seq, A_WIDTH).astype(a_v.dtype)
    return jax.nn.sigmoid(a_o) * h


def neighbourhood_attention(q, k, v, q_gain, k_gain, rpb):
    bsz, seq, _ = q.shape
    rows = seq // GRID_W
    win_r = min(NA_WIN_R, rows)

    def to_grid(a):
        return a.reshape(bsz, rows, GRID_W, B_HEADS, B_HEAD_DIM).transpose(0, 3, 1, 2, 4)

    q = rms_norm(q.reshape(bsz, seq, B_HEADS, B_HEAD_DIM), q_gain) * (B_HEAD_DIM ** -0.5)
    k = rms_norm(k.reshape(bsz, seq, B_HEADS, B_HEAD_DIM), k_gain)
    qg = jnp.moveaxis(to_grid(q), 2, 0)
    kg = to_grid(k)
    vg = to_grid(v)
    qc = jnp.arange(GRID_W)
    cs = jnp.clip(qc - NA_WIN_C // 2, 0, GRID_W - NA_WIN_C)
    col_mask = (qc[None, :] >= cs[:, None]) & (qc[None, :] < cs[:, None] + NA_WIN_C)
    dc_idx = jnp.clip(qc[None, :] - qc[:, None] + NA_WIN_C - 1, 0, 2 * NA_WIN_C - 2)

    def row_block(args):
        r, q_r = args
        rs = jnp.clip(r - win_r // 2, 0, rows - win_r)
        k_r = lax.dynamic_slice_in_dim(kg, rs, win_r, axis=2)
        v_r = lax.dynamic_slice_in_dim(vg, rs, win_r, axis=2)
        s = jnp.einsum('bhqd,bhjkd->bhqjk', q_r, k_r).astype(jnp.float32)
        dr = rs + jnp.arange(win_r) - r + NA_WIN_R - 1
        bias = rpb[:, dr[None, :, None], dc_idx[:, None, :]]
        s = jnp.where(col_mask[:, None, :], s + bias.astype(jnp.float32), -jnp.inf)
        p = jax.nn.softmax(s, axis=(-2, -1)).astype(v_r.dtype)
        return jnp.einsum('bhqjk,bhjkd->bhqd', p, v_r)

    out = lax.map(row_block, (jnp.arange(rows), qg))
    return out.transpose(1, 0, 3, 2, 4).reshape(bsz, seq, B_WIDTH)


def even_mixer(h, w_in0, a_conv_w, a_gate_b, a_norm_g, b_q_gain, b_k_gain, b_rpb, w_out0):
    proj = h @ w_in0
    offsets = np.cumsum(np.array(LAYER0_SPLITS))[:-1].tolist()
    a_q, a_k, a_v, a_o, a_z, a_gates, b_q, b_k, b_v, b_z = jnp.split(proj, offsets, axis=-1)
    y_a = mlstm_mixer(a_q, a_k, a_v, a_o, a_gates, a_conv_w, a_gate_b, a_norm_g) * jax.nn.silu(a_z)
    y_b = neighbourhood_attention(b_q, b_k, b_v, b_q_gain, b_k_gain, b_rpb) * jax.nn.silu(b_z)
    return jnp.concatenate([y_a, y_b], axis=-1) @ w_out0


def odd_mixer(h, w_in1, c_v_norm_g, c_w_s, c_b_s, w_out1):
    bsz, seq, _ = h.shape
    u, v, z = jnp.split(h @ w_in1, 3, axis=-1)
    u = jax.nn.gelu(u)
    v = rms_norm(jax.nn.gelu(v), c_v_norm_g)
    v = v.reshape(bsz, seq // C_CHUNK, C_CHUNK, C_GROUPS, C_WIDTH // C_GROUPS)
    sv = jnp.einsum('gts,bnsgc->bntgc', c_w_s, v) + c_b_s.T[:, :, None]
    y = u * sv.reshape(bsz, seq, C_WIDTH) * jax.nn.silu(z)
    return y @ w_out1


def setup_inputs(seed: int = 0) -> dict:
    key = jax.random.key(seed)
    ks = jax.random.split(key, 24)
    nrm = lambda k, shape, scale: jax.random.normal(k, shape, jnp.float32) * scale
    D = D_MODEL
    fg_bias = jnp.linspace(3.0, 6.0, A_HEADS, dtype=jnp.float32)
    a_gate_b = jnp.concatenate([
        nrm(ks[7], (A_HEADS,), 0.1),
        fg_bias + nrm(ks[8], (A_HEADS,), 0.1),
        nrm(ks[9], (A_HEADS,), 0.1),
        fg_bias + nrm(ks[10], (A_HEADS,), 0.1)])
    return {
        'x': nrm(ks[0], (BATCH, SEQ, D), 1.0),
        'c': nrm(ks[1], (BATCH, D), 1.0),
        'norm_g0': 1.0 + nrm(ks[2], (D,), 0.02),
        'ada_w0': nrm(ks[3], (D, 3 * D), D ** -0.5),
        'ada_b0': nrm(ks[4], (3 * D,), 0.02),
        'w_in0': nrm(ks[5], (D, LAYER0_COLS), D ** -0.5),
        'a_conv_w': nrm(ks[6], (A_CONV_W, 2 * A_QK_WIDTH), A_CONV_W ** -0.5),
        'a_gate_b': a_gate_b,
        'a_norm_g': 1.0 + nrm(ks[11], (A_HEADS, A_V_DIM), 0.02),
        'b_q_gain': 1.0 + nrm(ks[12], (B_HEAD_DIM,), 0.02),
        'b_k_gain': 1.0 + nrm(ks[13], (B_HEAD_DIM,), 0.02),
        'b_rpb': nrm(ks[14], (B_HEADS, 2 * NA_WIN_R - 1, 2 * NA_WIN_C - 1), 0.1),
        'w_out0': nrm(ks[15], (A_WIDTH + B_WIDTH, D), (A_WIDTH + B_WIDTH) ** -0.5),
        'norm_g1': 1.0 + nrm(ks[16], (D,), 0.02),
        'ada_w1': nrm(ks[17], (D, 3 * D), D ** -0.5),
        'ada_b1': nrm(ks[18], (3 * D,), 0.02),
        'w_in1': nrm(ks[19], (D, 3 * C_WIDTH), D ** -0.5),
        'c_v_norm_g': 1.0 + nrm(ks[20], (C_WIDTH,), 0.02),
        'c_w_s': nrm(ks[21], (C_GROUPS, C_CHUNK, C_CHUNK), C_CHUNK ** -0.5),
        'c_b_s': 1.0 + nrm(ks[22], (C_GROUPS, C_CHUNK), 0.02),
        'w_out1': nrm(ks[23], (C_WIDTH, D), C_WIDTH ** -0.5),
    }


def reference(x, c, norm_g0, ada_w0, ada_b0, w_in0, a_conv_w, a_gate_b, a_norm_g, b_q_gain, b_k_gain, b_rpb, w_out0,
              norm_g1, ada_w1, ada_b1, w_in1, c_v_norm_g, c_w_s, c_b_s, w_out1):
    norm_gs = (norm_g0, norm_g1)
    ada_ws = (ada_w0, ada_w1)
    ada_bs = (ada_b0, ada_b1)
    mixers = (
        lambda h: even_mixer(h, w_in0, a_conv_w, a_gate_b, a_norm_g, b_q_gain, b_k_gain, b_rpb, w_out0),
        lambda h: odd_mixer(h, w_in1, c_v_norm_g, c_w_s, c_b_s, w_out1),
    )
    for layer in range(DEPTH):
        shift, scale, gate = modulation(c, ada_ws[layer], ada_bs[layer])
        h = rms_norm(x, norm_gs[layer]) * (1.0 + scale) + shift
        x = x + gate * mixers[layer % 2](h)
    return x
```

```python
import functools

import jax
import jax.numpy as jnp
from jax import lax
from jax.experimental import pallas as pl
from jax.experimental.pallas import tpu as pltpu

F32 = jnp.float32
BF16 = jnp.bfloat16
HIGHEST = lax.Precision.HIGHEST

EPS = 1e-6
GRID_W = 64
MLSTM_CHUNK = 128
SPATIAL_CHUNK = 128
LANES = 128
V7X_VMEM_LIMIT_BYTES = 52 * 1024 * 1024


def _params(n_axes, vmem=V7X_VMEM_LIMIT_BYTES):
    return pltpu.CompilerParams(dimension_semantics=("arbitrary",) * n_axes, vmem_limit_bytes=vmem)


def _tile(dim, pref):
    t = min(dim, pref)
    assert dim % t == 0, (dim, pref)
    return t


def _silu(x):
    return x * jax.nn.sigmoid(x)


def _rms(x):
    return x * lax.rsqrt(jnp.mean(x * x, axis=-1, keepdims=True) + EPS)


def _ada_kernel(c_ref, w_ref, b_ref, o_ref):
    a = _silu(c_ref[...]).astype(BF16)
    o_ref[...] = jnp.dot(a, w_ref[...].astype(BF16), preferred_element_type=F32) + b_ref[...]


def _modulation(c_pad, w, b):
    rows, d = c_pad.shape
    n = w.shape[1]
    tn = _tile(n, 512)
    return pl.pallas_call(
        _ada_kernel,
        grid=(n // tn,),
        in_specs=[pl.BlockSpec((rows, d), lambda j: (0, 0)),
                  pl.BlockSpec((d, tn), lambda j: (0, j)),
                  pl.BlockSpec((1, tn), lambda j: (0, j))],
        out_specs=pl.BlockSpec((rows, tn), lambda j: (0, j)),
        out_shape=jax.ShapeDtypeStruct((rows, n), F32),
        compiler_params=_params(1),
        name="ada_modulation",
    )(c_pad, w, b.reshape(1, n))


def _norm_mod_kernel(x_ref, g_ref, sc_ref, sh_ref, o_ref):
    y = _rms(x_ref[0]) * g_ref[...]
    o_ref[0] = (y * (1.0 + sc_ref[0]) + sh_ref[0]).astype(BF16)


def _norm_mod(x, g, scale, shift):
    b, s, d = x.shape
    ts = _tile(s, 256)
    vec = pl.BlockSpec((1, 1, d), lambda i, j: (i, 0, 0))
    return pl.pallas_call(
        _norm_mod_kernel,
        grid=(b, s // ts),
        in_specs=[pl.BlockSpec((1, ts, d), lambda i, j: (i, j, 0)),
                  pl.BlockSpec((1, d), lambda i, j: (0, 0)), vec, vec],
        out_specs=pl.BlockSpec((1, ts, d), lambda i, j: (i, j, 0)),
        out_shape=jax.ShapeDtypeStruct((b, s, d), BF16),
        compiler_params=_params(2),
        name="norm_modulate",
    )(x, g.reshape(1, d), scale.reshape(b, 1, d), shift.reshape(b, 1, d))


def _mm_kernel(a_ref, w_ref, o_ref):
    o_ref[...] = jnp.dot(a_ref[...], w_ref[...], preferred_element_type=F32).astype(o_ref.dtype)


def _matmul(a, w, out_dtype, tm_pref=1024, tn_pref=1024, name="matmul"):
    m, k = a.shape
    n = w.shape[1]
    tm, tn = _tile(m, tm_pref), _tile(n, tn_pref)
    return pl.pallas_call(
        _mm_kernel,
        grid=(m // tm, n // tn),
        in_specs=[pl.BlockSpec((tm, k), lambda i, j: (i, 0)),
                  pl.BlockSpec((k, tn), lambda i, j: (0, j))],
        out_specs=pl.BlockSpec((tm, tn), lambda i, j: (i, j)),
        out_shape=jax.ShapeDtypeStruct((m, n), out_dtype),
        compiler_params=_params(2),
        name=name,
    )(a, w)


def _conv_kernel(x_ref, w_ref, o_ref, *, n_q_blocks, q_scale):
    x = x_ref[0].astype(F32)
    s = x.shape[0]
    row = lax.broadcasted_iota(jnp.int32, x.shape, 0)
    x_prev = jnp.where(row == 0, 0.0, pltpu.roll(x, 1, axis=0))
    x_next = jnp.where(row == s - 1, 0.0, pltpu.roll(x, s - 1, axis=0))
    w = w_ref[...]
    y = _silu(x_prev * w[0:1] + x * w[1:2] + x_next * w[2:3])
    scale = jnp.where(pl.program_id(1) < n_q_blocks, q_scale, 1.0).astype(F32)
    o_ref[0] = (y * scale).astype(BF16)


def _conv_silu(proj, conv_w, qk_width, dk):
    b, s, _ = proj.shape
    assert conv_w.shape[0] == 3, "centred depthwise conv is written for 3 taps"
    width = 2 * qk_width
    tc = _tile(qk_width, 256)
    kern = functools.partial(_conv_kernel, n_q_blocks=qk_width // tc, q_scale=float(dk) ** -0.5)
    return pl.pallas_call(
        kern,
        grid=(b, width // tc),
        in_specs=[pl.BlockSpec((1, s, tc), lambda i, j: (i, 0, j)),
                  pl.BlockSpec((3, tc), lambda i, j: (0, j))],
        out_specs=pl.BlockSpec((1, s, tc), lambda i, j: (i, 0, j)),
        out_shape=jax.ShapeDtypeStruct((b, s, width), BF16),
        compiler_params=_params(2),
        name="qk_conv_silu",
    )(proj, conv_w)


def _gates_kernel(x_ref, b_ref, ig_ref, pre_ref, suf_ref):
    nc, r, l = x_ref.shape[1:]
    x = x_ref[0] + b_ref[...][None]
    ig_ref[0] = x
    lf = (jnp.minimum(x, 0.0) - jnp.log1p(jnp.exp(-jnp.abs(x)))).reshape(nc * r, l)
    u = lax.broadcasted_iota(jnp.int32, (l, l), 0)
    t = lax.broadcasted_iota(jnp.int32, (l, l), 1)
    pre = jnp.dot(lf, (u <= t).astype(F32), precision=HIGHEST, preferred_element_type=F32)
    suf = jnp.dot(lf, (u >= t).astype(F32), precision=HIGHEST, preferred_element_type=F32)
    pre_ref[0] = pre.reshape(nc, r, l)
    suf_ref[0] = suf.reshape(nc, r, l)


def _gate_prep(graw, bias):
    b, nc, r, l = graw.shape
    spec = pl.BlockSpec((1, nc, r, l), lambda i: (i, 0, 0, 0))
    shape = jax.ShapeDtypeStruct(graw.shape, F32)
    return pl.pallas_call(
        _gates_kernel,
        grid=(b,),
        in_specs=[spec, pl.BlockSpec((r, 1), lambda i: (0, 0))],
        out_specs=[spec, spec, spec],
        out_shape=[shape, shape, shape],
        compiler_params=_params(1),
        name="mlstm_gate_prep",
    )(graw, bias.reshape(r, 1))


def _row_to_col(row_vec, eye):
    return jnp.sum(jnp.where(eye, row_vec, 0.0), axis=1, keepdims=True)


def _mlstm_chunk(q, k, v, v_aug, ig, g, tot, m, c_ref, *, causal):
    l = q.shape[0]
    dv = v.shape[1]
    r_i = lax.broadcasted_iota(jnp.int32, (l, l), 0)
    c_i = lax.broadcasted_iota(jnp.int32, (l, l), 1)
    eye = r_i == c_i
    g_col = _row_to_col(g, eye)
    d = g_col - g + ig
    d = jnp.where((c_i <= r_i) if causal else (c_i >= r_i), d, -jnp.inf)
    gm = g_col + m
    m_t = jnp.maximum(gm, jnp.max(d, axis=1, keepdims=True))
    p = jnp.exp(d - m_t)
    inter = jnp.exp(gm - m_t)
    s_ = lax.dot_general(q, k, (((1,), (1,)), ((), ())), preferred_element_type=F32) * p
    qc = jnp.dot(q, c_ref[...].astype(BF16), preferred_element_type=F32)
    num = jnp.dot(s_.astype(BF16), v, preferred_element_type=F32) + inter * qc[:, :dv]
    den = jnp.sum(s_, axis=1, keepdims=True) + inter * qc[:, dv:dv + 1]
    h = num * (1.0 / jnp.maximum(jnp.abs(den), jnp.exp(-m_t)))
    ds = tot - g + ig
    m_new = jnp.maximum(tot + m, jnp.max(ds, axis=1, keepdims=True))
    w_col = _row_to_col(jnp.exp(ds - m_new), eye)
    decay = jnp.exp(tot + m - m_new)
    wk = (w_col * k.astype(F32)).astype(BF16)
    kv = lax.dot_general(wk, v_aug, (((0,), (0,)), ((), ())), preferred_element_type=F32)
    c_ref[...] = decay * c_ref[...] + kv
    return h, m_new


def _mlstm_kernel(q_ref, k_ref, v_ref, gp_ref, ng_ref, o_ref, acc_ref, cf_ref, cb_ref):
    s, dv = v_ref.shape[1:]
    l = MLSTM_CHUNK
    nc = s // l
    cf_ref[...] = jnp.zeros_like(cf_ref)
    cb_ref[...] = jnp.zeros_like(cb_ref)
    ones_col = (lax.broadcasted_iota(jnp.int32, (l, LANES), 1) == 0).astype(BF16)
    gain = ng_ref[0]

    def direction(c, m, c_ref, causal):
        rows = pl.ds(pl.multiple_of(c * l, l), l)
        q, k, v = q_ref[0, rows, :], k_ref[0, rows, :], v_ref[0, rows, :]
        gp = gp_ref[0, 0, c]
        if causal:
            ig, g = gp[0:1], gp[1:2]
            tot = g[:, l - 1:l]
        else:
            ig, g = gp[2:3], gp[3:4]
            tot = g[:, 0:1]
        v_aug = jnp.concatenate([v, ones_col], axis=1)
        h, m_new = _mlstm_chunk(q, k, v, v_aug, ig, g, tot, m, c_ref, causal=causal)
        return rows, h, m_new

    def emit(rows, h, first_touch):
        if first_touch:
            acc_ref[rows, :] = h
        else:
            o_ref[0, rows, :] = (_rms(acc_ref[rows, :] + h) * gain).astype(o_ref.dtype)

    def body(first_touch, i, carry):
        m_f, m_b = carry
        rows_f, h_f, m_f = direction(i, m_f, cf_ref, True)
        emit(rows_f, h_f, first_touch)
        rows_b, h_b, m_b = direction(nc - 1 - i, m_b, cb_ref, False)
        emit(rows_b, h_b, first_touch)
        return m_f, m_b

    zero = jnp.zeros((1, 1), F32)
    carry = lax.fori_loop(0, nc // 2, functools.partial(body, True), (zero, zero))
    lax.fori_loop(nc // 2, nc, functools.partial(body, False), carry)


def _mlstm(qk, proj, gate_pack, norm_g, heads, dk, dv, v_col0):
    b, s, _ = proj.shape
    assert s % (2 * MLSTM_CHUNK) == 0 and v_col0 % dv == 0
    nc = s // MLSTM_CHUNK
    return pl.pallas_call(
        _mlstm_kernel,
        grid=(b, heads),
        in_specs=[pl.BlockSpec((1, s, dk), lambda i, h: (i, 0, h)),
                  pl.BlockSpec((1, s, dk), lambda i, h: (i, 0, heads + h)),
                  pl.BlockSpec((1, s, dv), lambda i, h: (i, 0, v_col0 // dv + h)),
                  pl.BlockSpec((1, 1, nc, 8, MLSTM_CHUNK), lambda i, h: (i, h, 0, 0, 0)),
                  pl.BlockSpec((1, 1, dv), lambda i, h: (h, 0, 0))],
        out_specs=pl.BlockSpec((1, s, dv), lambda i, h: (i, 0, h)),
        out_shape=jax.ShapeDtypeStruct((b, s, heads * dv), BF16),
        scratch_shapes=[pltpu.VMEM((s, dv), F32),
                        pltpu.VMEM((dk, dv + LANES), F32),
                        pltpu.VMEM((dk, dv + LANES), F32)],
        compiler_params=_params(2),
        name="mlstm_bidirectional",
    )(qk, qk, proj, gate_pack, norm_g.reshape(heads, 1, dv))


def _gate_a_kernel(h_ref, o_ref_in, z_ref, y_ref):
    o = o_ref_in[0].astype(F32)
    z = z_ref[0].astype(F32)
    y_ref[0] = (jax.nn.sigmoid(o) * h_ref[0].astype(F32) * _silu(z)).astype(BF16)


def _gate_a(hn, proj, o_col0, z_col0):
    b, s, width = hn.shape
    ts, tc = _tile(s, 512), _tile(width, 1024)
    assert o_col0 % tc == 0 and z_col0 % tc == 0
    return pl.pallas_call(
        _gate_a_kernel,
        grid=(b, s // ts, width // tc),
        in_specs=[pl.BlockSpec((1, ts, tc), lambda i, j, c: (i, j, c)),
                  pl.BlockSpec((1, ts, tc), lambda i, j, c: (i, j, o_col0 // tc + c)),
                  pl.BlockSpec((1, ts, tc), lambda i, j, c: (i, j, z_col0 // tc + c))],
        out_specs=pl.BlockSpec((1, ts, tc), lambda i, j, c: (i, j, c)),
        out_shape=jax.ShapeDtypeStruct((b, s, width), BF16),
        compiler_params=_params(3),
        name="mlstm_out_gate",
    )(hn, proj, proj)


def _bias_table_kernel(r_ref, o_ref, *, win_r, win_c, dpad):
    qc = pl.program_id(0)
    n_rows, n_cols = r_ref.shape[1], o_ref.shape[1]
    rr = lax.broadcasted_iota(jnp.int32, (n_rows, n_cols), 0)
    cc = lax.broadcasted_iota(jnp.int32, (n_rows, n_cols), 1)
    kc = cc & (GRID_W - 1)
    dc = jnp.clip(kc - qc + (win_c - 1), 0, 2 * win_c - 2)
    same_j = (rr >> (dpad.bit_length() - 1)) == (cc >> (GRID_W.bit_length() - 1))
    onehot = jnp.where(same_j & ((rr & (dpad - 1)) == dc), 1.0, 0.0).astype(F32)
    vals = jnp.dot(r_ref[...], onehot, precision=HIGHEST, preferred_element_type=F32)
    kc_o = lax.broadcasted_iota(jnp.int32, vals.shape, 1) & (GRID_W - 1)
    cs = jnp.clip(qc - win_c // 2, 0, GRID_W - win_c)
    o_ref[...] = jnp.where((kc_o >= cs) & (kc_o < cs + win_c), vals, -jnp.inf)


def _bias_table(rpb):
    hb, n_dr, n_dc = rpb.shape
    win_r, win_c = (n_dr + 1) // 2, (n_dc + 1) // 2
    dpad = 32
    assert n_dc <= dpad
    padded = jnp.pad(rpb, ((0, 0), (0, 0), (0, dpad - n_dc)))
    windows = jnp.stack([padded[:, d:d + win_r] for d in range(win_r)], axis=1)
    r = windows.reshape(hb * win_r, win_r * dpad)
    n_cols = win_r * GRID_W
    kern = functools.partial(_bias_table_kernel, win_r=win_r, win_c=win_c, dpad=dpad)
    table = pl.pallas_call(
        kern,
        grid=(GRID_W,),
        in_specs=[pl.BlockSpec(r.shape, lambda q: (0, 0))],
        out_specs=pl.BlockSpec((r.shape[0], n_cols), lambda q: (0, q)),
        out_shape=jax.ShapeDtypeStruct((r.shape[0], GRID_W * n_cols), F32),
        compiler_params=_params(1),
        name="natten_bias_table",
    )(r)
    return table.reshape(hb, win_r, GRID_W, n_cols), win_r


def _natten_kernel(q_ref, k_ref, v_ref, z_ref, bias_ref, qg_ref, kg_ref, o_ref, qn_ref, kn_ref, *, win_r):
    s, hd = q_ref.shape[1:]
    w = GRID_W
    n_rows = s // w
    qn_ref[...] = (_rms(q_ref[0].astype(F32)) * qg_ref[...] * (float(hd) ** -0.5)).astype(BF16)
    kn_ref[...] = (_rms(k_ref[0].astype(F32)) * kg_ref[...]).astype(BF16)

    def row(r, carry):
        rs = jnp.clip(r - win_r // 2, 0, n_rows - win_r)
        q_rows = pl.ds(pl.multiple_of(r * w, w), w)
        kv_rows = pl.ds(pl.multiple_of(rs * w, w), win_r * w)
        sc = lax.dot_general(qn_ref[q_rows, :], kn_ref[kv_rows, :], (((1,), (1,)), ((), ())),
                             preferred_element_type=F32)
        sc = sc + bias_ref[0, rs - r + win_r - 1]
        p = jnp.exp(sc - jnp.max(sc, axis=1, keepdims=True))
        o = jnp.dot(p.astype(BF16), v_ref[0, kv_rows, :], preferred_element_type=F32)
        o = o * (1.0 / jnp.sum(p, axis=1, keepdims=True))
        o_ref[0, q_rows, :] = (o * _silu(z_ref[0, q_rows, :].astype(F32))).astype(o_ref.dtype)
        return carry

    lax.fori_loop(0, n_rows, row, 0)


def _natten(proj, bias_table, win_r, q_gain, k_gain, heads, hd, col0):
    b, s, _ = proj.shape
    assert s % GRID_W == 0 and s // GRID_W >= win_r and col0 % hd == 0
    c0 = col0 // hd

    def col(part):
        return pl.BlockSpec((1, s, hd), lambda i, h: (i, 0, c0 + part * heads + h))

    vec = pl.BlockSpec((1, hd), lambda i, h: (0, 0))
    kern = functools.partial(_natten_kernel, win_r=win_r)
    return pl.pallas_call(
        kern,
        grid=(b, heads),
        in_specs=[col(0), col(1), col(2), col(3),
                  pl.BlockSpec((1,) + bias_table.shape[1:], lambda i, h: (h, 0, 0, 0)), vec, vec],
        out_specs=pl.BlockSpec((1, s, hd), lambda i, h: (i, 0, h)),
        out_shape=jax.ShapeDtypeStruct((b, s, heads * hd), BF16),
        scratch_shapes=[pltpu.VMEM((s, hd), BF16), pltpu.VMEM((s, hd), BF16)],
        compiler_params=_params(2),
        name="neighbourhood_attention",
    )(proj, proj, proj, proj, bias_table, q_gain.reshape(1, hd), k_gain.reshape(1, hd))


def _mm_out_kernel(a1_ref, a2_ref, w1_ref, w2_ref, x_ref, gate_ref, o_ref):
    acc = jnp.dot(a1_ref[0], w1_ref[...], preferred_element_type=F32)
    acc = acc + jnp.dot(a2_ref[0], w2_ref[...], preferred_element_type=F32)
    o_ref[0] = x_ref[0] + gate_ref[0] * acc


def _out_proj(a1, a1_blk, a2, a2_blk, w, x, gate, name):
    b, s, d = x.shape
    kh = w.shape[0] // 2
    tm, tn = _tile(s, 512), _tile(d, 512)
    return pl.pallas_call(
        _mm_out_kernel,
        grid=(b, s // tm, d // tn),
        in_specs=[pl.BlockSpec((1, tm, kh), lambda i, m, n: (i, m, a1_blk)),
                  pl.BlockSpec((1, tm, kh), lambda i, m, n: (i, m, a2_blk)),
                  pl.BlockSpec((kh, tn), lambda i, m, n: (0, n)),
                  pl.BlockSpec((kh, tn), lambda i, m, n: (1, n)),
                  pl.BlockSpec((1, tm, tn), lambda i, m, n: (i, m, n)),
                  pl.BlockSpec((1, 1, tn), lambda i, m, n: (i, 0, n))],
        out_specs=pl.BlockSpec((1, tm, tn), lambda i, m, n: (i, m, n)),
        out_shape=jax.ShapeDtypeStruct((b, s, d), F32),
        compiler_params=_params(3),
        name=name,
    )(a1, a2, w, w, x, gate.reshape(b, 1, d))


def _mm_glu_kernel(a_ref, wu_ref, wv_ref, wz_ref, uz_ref, gv_ref):
    a = a_ref[...]
    u = jnp.dot(a, wu_ref[...], preferred_element_type=F32)
    z = jnp.dot(a, wz_ref[...], preferred_element_type=F32)
    uz_ref[...] = (jax.nn.gelu(u) * _silu(z)).astype(BF16)
    v = jnp.dot(a, wv_ref[...], preferred_element_type=F32)
    gv_ref[...] = jax.nn.gelu(v).astype(BF16)


def _glu_proj(a, w, cw):
    m, k = a.shape
    tm, tn = _tile(m, 1024), _tile(cw, 256)
    nj = cw // tn
    out = jax.ShapeDtypeStruct((m, cw), BF16)
    ospec = pl.BlockSpec((tm, tn), lambda i, j: (i, j))
    return pl.pallas_call(
        _mm_glu_kernel,
        grid=(m // tm, nj),
        in_specs=[pl.BlockSpec((tm, k), lambda i, j: (i, 0)),
                  pl.BlockSpec((k, tn), lambda i, j: (0, j)),
                  pl.BlockSpec((k, tn), lambda i, j: (0, nj + j)),
                  pl.BlockSpec((k, tn), lambda i, j: (0, 2 * nj + j))],
        out_specs=[ospec, ospec],
        out_shape=[out, out],
        compiler_params=_params(2),
        name="gmlp_in_proj",
    )(a, w, w, w)


def _spatial_kernel(gv_ref, uz_ref, ws_ref, bs_ref, vg_ref, o_ref):
    groups = ws_ref.shape[0]
    cg = gv_ref.shape[2] // groups
    vn = (_rms(gv_ref[0].astype(F32)) * vg_ref[...]).astype(BF16)
    for g in range(groups):
        cols = slice(g * cg, (g + 1) * cg)
        sv = jnp.dot(ws_ref[g], vn[:, cols], preferred_element_type=F32) + bs_ref[:, g:g + 1]
        o_ref[0, :, cols] = (uz_ref[0, :, cols].astype(F32) * sv).astype(BF16)


def _spatial(gv, uz, ws, bs_t, v_gain):
    b, s, cw = gv.shape
    l = SPATIAL_CHUNK
    groups = ws.shape[0]
    assert s % l == 0 and ws.shape[1:] == (l, l)
    blk = pl.BlockSpec((1, l, cw), lambda i, n: (i, n, 0))
    return pl.pallas_call(
        _spatial_kernel,
        grid=(b, s // l),
        in_specs=[blk, blk,
                  pl.BlockSpec(ws.shape, lambda i, n: (0, 0, 0)),
                  pl.BlockSpec((l, groups), lambda i, n: (0, 0)),
                  pl.BlockSpec((1, cw), lambda i, n: (0, 0))],
        out_specs=blk,
        out_shape=jax.ShapeDtypeStruct((b, s, cw), BF16),
        compiler_params=_params(2),
        name="gmlp_spatial_gate",
    )(gv, uz, ws, bs_t, v_gain.reshape(1, cw))


def kernel(x, c, norm_g0, ada_w0, ada_b0, w_in0, a_conv_w, a_gate_b, a_norm_g, b_q_gain, b_k_gain, b_rpb, w_out0,
           norm_g1, ada_w1, ada_b1, w_in1, c_v_norm_g, c_w_s, c_b_s, w_out1):
    b, s, d = x.shape
    a_heads, dv = a_norm_g.shape
    a_width = a_heads * dv
    dk = dv // 2
    qk_width = a_heads * dk
    hd = b_q_gain.shape[0]
    b_heads, b_width = b_rpb.shape[0], b_rpb.shape[0] * hd
    n_gate = 4 * a_heads
    a_cols = 2 * qk_width + 3 * a_width
    assert w_in0.shape[1] == a_cols + n_gate + 4 * b_width
    cw = c_v_norm_g.shape[0]
    nc = s // MLSTM_CHUNK

    c_pad = jnp.pad(c, ((0, -b % 8), (0, 0)))
    mod0 = _modulation(c_pad, ada_w0, ada_b0)[:b]
    mod1 = _modulation(c_pad, ada_w1, ada_b1)[:b]
    shift0, scale0, gate0 = mod0[:, :d], mod0[:, d:2 * d], mod0[:, 2 * d:]
    shift1, scale1, gate1 = mod1[:, :d], mod1[:, d:2 * d], mod1[:, 2 * d:]

    h0 = _norm_mod(x, norm_g0, scale0, shift0).reshape(b * s, d)
    w_main = jnp.concatenate([w_in0[:, :a_cols], w_in0[:, a_cols + n_gate:]], axis=1).astype(BF16)
    w_gate = jnp.pad(w_in0[:, a_cols:a_cols + n_gate], ((0, 0), (0, LANES - n_gate))).astype(BF16)
    proj = _matmul(h0, w_main, BF16, name="layer0_in_proj").reshape(b, s, a_cols + 4 * b_width)
    gates = _matmul(h0, w_gate, F32, name="layer0_gate_proj")[:, :n_gate]

    qk = _conv_silu(proj, a_conv_w, qk_width, dk)
    graw = gates.reshape(b, nc, MLSTM_CHUNK, n_gate).transpose(0, 1, 3, 2)
    ig, pre, suf = _gate_prep(graw, a_gate_b)
    hh = a_heads
    zeros = jnp.zeros_like(ig[:, :, :hh])
    pack = jnp.stack([ig[:, :, :hh], pre[:, :, hh:2 * hh], ig[:, :, 2 * hh:3 * hh], suf[:, :, 3 * hh:],
                      zeros, zeros, zeros, zeros], axis=3)
    pack = pack.transpose(0, 2, 1, 3, 4)
    hn = _mlstm(qk, proj, pack, a_norm_g, a_heads, dk, dv, 2 * qk_width)
    y_a = _gate_a(hn, proj, 2 * qk_width + a_width, 2 * qk_width + 2 * a_width)

    table, win_r = _bias_table(b_rpb)
    y_b = _natten(proj, table, win_r, b_q_gain, b_k_gain, b_heads, hd, a_cols)

    assert a_width == b_width
    x1 = _out_proj(y_a, 0, y_b, 0, w_out0.astype(BF16), x, gate0, "layer0_out_proj")

    h1 = _norm_mod(x1, norm_g1, scale1, shift1).reshape(b * s, d)
    uz, gv = _glu_proj(h1, w_in1.astype(BF16), cw)
    y = _spatial(gv.reshape(b, s, cw), uz.reshape(b, s, cw), c_w_s.astype(BF16), c_b_s.T, c_v_norm_g)
    return _out_proj(y, 0, y, 1, w_out1.astype(BF16), x1, gate1, "layer1_out_proj")
```

```python
import functools

import jax
import jax.numpy as jnp
from jax import lax
from jax.experimental import pallas as pl
from jax.experimental.pallas import tpu as pltpu

F32 = jnp.float32
BF16 = jnp.bfloat16
HIGHEST = lax.Precision.HIGHEST

EPS = 1e-6
GRID_W = 64
MLSTM_CHUNK = 128
SPATIAL_CHUNK = 128
LANES = 128
V7X_VMEM_LIMIT_BYTES = 52 * 1024 * 1024


def _params(n_axes, vmem=V7X_VMEM_LIMIT_BYTES):
    return pltpu.CompilerParams(dimension_semantics=("arbitrary",) * n_axes, vmem_limit_bytes=vmem)


def _tile(dim, pref):
    t = min(dim, pref)
    assert dim % t == 0, (dim, pref)
    return t


def _silu(x):
    return x * jax.nn.sigmoid(x)


def _rms(x):
    return x * lax.rsqrt(jnp.mean(x * x, axis=-1, keepdims=True) + EPS)


def _ada_kernel(c_ref, w_ref, b_ref, o_ref):
    a = _silu(c_ref[...]).astype(BF16)
    o_ref[...] = jnp.dot(a, w_ref[...].astype(BF16), preferred_element_type=F32) + b_ref[...]


def _modulation(c_pad, w, b):
    rows, d = c_pad.shape
    n = w.shape[1]
    tn = _tile(n, 512)
    return pl.pallas_call(
        _ada_kernel,
        grid=(n // tn,),
        in_specs=[pl.BlockSpec((rows, d), lambda j: (0, 0)),
                  pl.BlockSpec((d, tn), lambda j: (0, j)),
                  pl.BlockSpec((1, tn), lambda j: (0, j))],
        out_specs=pl.BlockSpec((rows, tn), lambda j: (0, j)),
        out_shape=jax.ShapeDtypeStruct((rows, n), F32),
        compiler_params=_params(1),
        name="ada_modulation",
    )(c_pad, w, b.reshape(1, n))


def _norm_mod_kernel(x_ref, g_ref, sc_ref, sh_ref, o_ref):
    y = _rms(x_ref[0]) * g_ref[...]
    o_ref[0] = (y * (1.0 + sc_ref[0]) + sh_ref[0]).astype(BF16)


def _norm_mod(x, g, scale, shift):
    b, s, d = x.shape
    ts = _tile(s, 256)
    vec = pl.BlockSpec((1, 1, d), lambda i, j: (i, 0, 0))
    return pl.pallas_call(
        _norm_mod_kernel,
        grid=(b, s // ts),
        in_specs=[pl.BlockSpec((1, ts, d), lambda i, j: (i, j, 0)),
                  pl.BlockSpec((1, d), lambda i, j: (0, 0)), vec, vec],
        out_specs=pl.BlockSpec((1, ts, d), lambda i, j: (i, j, 0)),
        out_shape=jax.ShapeDtypeStruct((b, s, d), BF16),
        compiler_params=_params(2),
        name="norm_modulate",
    )(x, g.reshape(1, d), scale.reshape(b, 1, d), shift.reshape(b, 1, d))


def _mm_kernel(a_ref, w_ref, o_ref):
    o_ref[...] = jnp.dot(a_ref[...], w_ref[...], preferred_element_type=F32).astype(o_ref.dtype)


def _matmul(a, w, out_dtype, tm_pref=1024, tn_pref=1024, name="matmul"):
    m, k = a.shape
    n = w.shape[1]
    tm, tn = _tile(m, tm_pref), _tile(n, tn_pref)
    return pl.pallas_call(
        _mm_kernel,
        grid=(m // tm, n // tn),
        in_specs=[pl.BlockSpec((tm, k), lambda i, j: (i, 0)),
                  pl.BlockSpec((k, tn), lambda i, j: (0, j))],
        out_specs=pl.BlockSpec((tm, tn), lambda i, j: (i, j)),
        out_shape=jax.ShapeDtypeStruct((m, n), out_dtype),
        compiler_params=_params(2),
        name=name,
    )(a, w)


def _conv_kernel(x_ref, w_ref, o_ref, *, n_q_blocks, q_scale):
    x = x_ref[0].astype(F32)
    s = x.shape[0]
    row = lax.broadcasted_iota(jnp.int32, x.shape, 0)
    x_prev = jnp.where(row == 0, 0.0, pltpu.roll(x, 1, axis=0))
    x_next = jnp.where(row == s - 1, 0.0, pltpu.roll(x, s - 1, axis=0))
    w = w_ref[...]
    y = _silu(x_prev * w[0:1] + x * w[1:2] + x_next * w[2:3])
    scale = jnp.where(pl.program_id(1) < n_q_blocks, q_scale, 1.0).astype(F32)
    o_ref[0] = (y * scale).astype(BF16)


def _conv_silu(proj, conv_w, qk_width, dk):
    b, s, _ = proj.shape
    assert conv_w.shape[0] == 3, "centred depthwise conv is written for 3 taps"
    width = 2 * qk_width
    tc = _tile(qk_width, 256)
    kern = functools.partial(_conv_kernel, n_q_blocks=qk_width // tc, q_scale=float(dk) ** -0.5)
    return pl.pallas_call(
        kern,
        grid=(b, width // tc),
        in_specs=[pl.BlockSpec((1, s, tc), lambda i, j: (i, 0, j)),
                  pl.BlockSpec((3, tc), lambda i, j: (0, j))],
        out_specs=pl.BlockSpec((1, s, tc), lambda i, j: (i, 0, j)),
        out_shape=jax.ShapeDtypeStruct((b, s, width), BF16),
        compiler_params=_params(2),
        name="qk_conv_silu",
    )(proj, conv_w)


def _gates_kernel(x_ref, b_ref, ig_ref, pre_ref, suf_ref):
    nc, r, l = x_ref.shape[1:]
    x = x_ref[0] + b_ref[...][None]
    ig_ref[0] = x
    lf = (jnp.minimum(x, 0.0) - jnp.log1p(jnp.exp(-jnp.abs(x)))).reshape(nc * r, l)
    u = lax.broadcasted_iota(jnp.int32, (l, l), 0)
    t = lax.broadcasted_iota(jnp.int32, (l, l), 1)
    pre = jnp.dot(lf, (u <= t).astype(F32), precision=HIGHEST, preferred_element_type=F32)
    suf = jnp.dot(lf, (u >= t).astype(F32), precision=HIGHEST, preferred_element_type=F32)
    pre_ref[0] = pre.reshape(nc, r, l)
    suf_ref[0] = suf.reshape(nc, r, l)


def _gate_prep(graw, bias):
    b, nc, r, l = graw.shape
    spec = pl.BlockSpec((1, nc, r, l), lambda i: (i, 0, 0, 0))
    shape = jax.ShapeDtypeStruct(graw.shape, F32)
    return pl.pallas_call(
        _gates_kernel,
        grid=(b,),
        in_specs=[spec, pl.BlockSpec((r, 1), lambda i: (0, 0))],
        out_specs=[spec, spec, spec],
        out_shape=[shape, shape, shape],
        compiler_params=_params(1),
        name="mlstm_gate_prep",
    )(graw, bias.reshape(r, 1))


def _row_to_col(row_vec, eye):
    return jnp.sum(jnp.where(eye, row_vec, 0.0), axis=1, keepdims=True)


def _mlstm_chunk(q, k, v, ig, g, tot, m, n, c_ref, *, causal):
    l = q.shape[0]
    r_i = lax.broadcasted_iota(jnp.int32, (l, l), 0)
    c_i = lax.broadcasted_iota(jnp.int32, (l, l), 1)
    eye = r_i == c_i
    g_col = _row_to_col(g, eye)
    d = g_col - g + ig
    d = jnp.where((c_i <= r_i) if causal else (c_i >= r_i), d, -jnp.inf)
    gm = g_col + m
    m_t = jnp.maximum(gm, jnp.max(d, axis=1, keepdims=True))
    p = jnp.exp(d - m_t)
    inter = jnp.exp(gm - m_t)
    s_ = lax.dot_general(q, k, (((1,), (1,)), ((), ())), preferred_element_type=F32) * p
    qc = jnp.dot(q, c_ref[...].astype(BF16), preferred_element_type=F32)
    num = jnp.dot(s_.astype(BF16), v, preferred_element_type=F32) + inter * qc
    qn = jnp.sum(q.astype(F32) * n, axis=1, keepdims=True)
    den = jnp.sum(s_, axis=1, keepdims=True) + inter * qn
    h = num * (1.0 / jnp.maximum(jnp.abs(den), jnp.exp(-m_t)))
    ds = tot - g + ig
    m_new = jnp.maximum(tot + m, jnp.max(ds, axis=1, keepdims=True))
    w_col = _row_to_col(jnp.exp(ds - m_new), eye)
    decay = jnp.exp(tot + m - m_new)
    wk = w_col * k.astype(F32)
    kv = lax.dot_general(wk.astype(BF16), v, (((0,), (0,)), ((), ())), preferred_element_type=F32)
    c_ref[...] = decay * c_ref[...] + kv
    n_new = decay * n + jnp.sum(wk, axis=0, keepdims=True)
    return h, m_new, n_new


def _mlstm_kernel(q_ref, k_ref, v_ref, gp_ref, ng_ref, o_ref, acc_ref, cf_ref, cb_ref):
    s, dv = v_ref.shape[1:]
    dk = q_ref.shape[2]
    l = MLSTM_CHUNK
    nc = s // l
    cf_ref[...] = jnp.zeros_like(cf_ref)
    cb_ref[...] = jnp.zeros_like(cb_ref)
    gain = ng_ref[0]

    def direction(c, state, c_ref, causal):
        m, n = state
        rows = pl.ds(pl.multiple_of(c * l, l), l)
        q, k, v = q_ref[0, rows, :], k_ref[0, rows, :], v_ref[0, rows, :]
        gp = gp_ref[0, 0, c]
        if causal:
            ig, g = gp[0:1], gp[1:2]
            tot = g[:, l - 1:l]
        else:
            ig, g = gp[2:3], gp[3:4]
            tot = g[:, 0:1]
        h, m_new, n_new = _mlstm_chunk(q, k, v, ig, g, tot, m, n, c_ref, causal=causal)
        return rows, h, (m_new, n_new)

    def emit(rows, h, first_touch):
        if first_touch:
            acc_ref[rows, :] = h
        else:
            o_ref[0, rows, :] = (_rms(acc_ref[rows, :] + h) * gain).astype(o_ref.dtype)

    def body(first_touch, i, carry):
        st_f, st_b = carry
        rows_f, h_f, st_f = direction(i, st_f, cf_ref, True)
        emit(rows_f, h_f, first_touch)
        rows_b, h_b, st_b = direction(nc - 1 - i, st_b, cb_ref, False)
        emit(rows_b, h_b, first_touch)
        return st_f, st_b

    init = (jnp.zeros((1, 1), F32), jnp.zeros((1, dk), F32))
    carry = lax.fori_loop(0, nc // 2, functools.partial(body, True), (init, init))
    lax.fori_loop(nc // 2, nc, functools.partial(body, False), carry)


def _mlstm(qk, proj, gate_pack, norm_g, heads, dk, dv, v_col0):
    b, s, _ = proj.shape
    assert s % (2 * MLSTM_CHUNK) == 0 and v_col0 % dv == 0
    nc = s // MLSTM_CHUNK
    return pl.pallas_call(
        _mlstm_kernel,
        grid=(b, heads),
        in_specs=[pl.BlockSpec((1, s, dk), lambda i, h: (i, 0, h)),
                  pl.BlockSpec((1, s, dk), lambda i, h: (i, 0, heads + h)),
                  pl.BlockSpec((1, s, dv), lambda i, h: (i, 0, v_col0 // dv + h)),
                  pl.BlockSpec((1, 1, nc, 8, MLSTM_CHUNK), lambda i, h: (i, h, 0, 0, 0)),
                  pl.BlockSpec((1, 1, dv), lambda i, h: (h, 0, 0))],
        out_specs=pl.BlockSpec((1, s, dv), lambda i, h: (i, 0, h)),
        out_shape=jax.ShapeDtypeStruct((b, s, heads * dv), BF16),
        scratch_shapes=[pltpu.VMEM((s, dv), F32),
                        pltpu.VMEM((dk, dv), F32),
                        pltpu.VMEM((dk, dv), F32)],
        compiler_params=_params(2),
        name="mlstm_bidirectional",
    )(qk, qk, proj, gate_pack, norm_g.reshape(heads, 1, dv))


def _gate_a_kernel(h_ref, o_ref_in, z_ref, y_ref):
    o = o_ref_in[0].astype(F32)
    z = z_ref[0].astype(F32)
    y_ref[0] = (jax.nn.sigmoid(o) * h_ref[0].astype(F32) * _silu(z)).astype(BF16)


def _gate_a(hn, proj, o_col0, z_col0):
    b, s, width = hn.shape
    ts, tc = _tile(s, 512), _tile(width, 1024)
    assert o_col0 % tc == 0 and z_col0 % tc == 0
    return pl.pallas_call(
        _gate_a_kernel,
        grid=(b, s // ts, width // tc),
        in_specs=[pl.BlockSpec((1, ts, tc), lambda i, j, c: (i, j, c)),
                  pl.BlockSpec((1, ts, tc), lambda i, j, c: (i, j, o_col0 // tc + c)),
                  pl.BlockSpec((1, ts, tc), lambda i, j, c: (i, j, z_col0 // tc + c))],
        out_specs=pl.BlockSpec((1, ts, tc), lambda i, j, c: (i, j, c)),
        out_shape=jax.ShapeDtypeStruct((b, s, width), BF16),
        compiler_params=_params(3),
        name="mlstm_out_gate",
    )(hn, proj, proj)


def _rpb_toeplitz_kernel(r_ref, o_ref, *, win_c):
    dpad, n = r_ref.shape[1], o_ref.shape[1]
    dd = lax.broadcasted_iota(jnp.int32, (dpad, n), 0)
    cc = lax.broadcasted_iota(jnp.int32, (dpad, n), 1)
    shift = GRID_W.bit_length() - 1
    qc, kc = cc >> shift, cc & (GRID_W - 1)
    dc = jnp.clip(kc - qc + (win_c - 1), 0, 2 * win_c - 2)
    onehot = jnp.where(dd == dc, 1.0, 0.0).astype(F32)
    vals = jnp.dot(r_ref[...], onehot, precision=HIGHEST, preferred_element_type=F32)
    co = lax.broadcasted_iota(jnp.int32, (1, n), 1)
    qo, ko = co >> shift, co & (GRID_W - 1)
    cs = jnp.clip(qo - win_c // 2, 0, GRID_W - win_c)
    o_ref[...] = jnp.where((ko >= cs) & (ko < cs + win_c), vals, -jnp.inf)


def _natten_plan(n_rows, win_r, q_rows, k_rows):
    variants, var_idx, k_start = [], [], []
    for r0 in range(0, n_rows, q_rows):
        kp = min(max(r0 - win_r // 2, 0), n_rows - k_rows)
        pattern = []
        for i in range(q_rows):
            r = r0 + i
            rs = min(max(r - win_r // 2, 0), n_rows - win_r)
            assert kp <= rs and rs + win_r <= kp + k_rows
            pattern.append(tuple((kp + jj - r + win_r - 1) if rs <= kp + jj < rs + win_r else None
                                 for jj in range(k_rows)))
        pattern = tuple(pattern)
        if pattern not in variants:
            variants.append(pattern)
        var_idx.append(variants.index(pattern))
        k_start.append(kp)
    return variants, var_idx, k_start


def _bias_assemble_kernel(t_ref, o_ref, *, variants):
    w = GRID_W
    masked = jnp.full((w, w), -jnp.inf, F32)
    for v, pattern in enumerate(variants):
        for i, row in enumerate(pattern):
            tiles = [masked if dr is None else t_ref[0, dr] for dr in row]
            o_ref[0, v, i * w:(i + 1) * w, :] = jnp.concatenate(tiles, axis=1)


def _natten_bias(rpb, variants):
    hb, n_dr, n_dc = rpb.shape
    win_c = (n_dc + 1) // 2
    w = GRID_W
    dpad = 32
    assert n_dc <= dpad and (hb * n_dr) % 8 == 0
    r = jnp.pad(rpb, ((0, 0), (0, 0), (0, dpad - n_dc))).reshape(hb * n_dr, dpad)
    toeplitz = pl.pallas_call(
        functools.partial(_rpb_toeplitz_kernel, win_c=win_c),
        grid=(1,),
        in_specs=[pl.BlockSpec(r.shape, lambda i: (0, 0))],
        out_specs=pl.BlockSpec((r.shape[0], w * w), lambda i: (0, 0)),
        out_shape=jax.ShapeDtypeStruct((r.shape[0], w * w), F32),
        compiler_params=_params(1),
        name="natten_rpb_toeplitz",
    )(r).reshape(hb, n_dr, w, w)
    q_rows, k_rows = len(variants[0]), len(variants[0][0])
    out_blk = (1, len(variants), q_rows * w, k_rows * w)
    return pl.pallas_call(
        functools.partial(_bias_assemble_kernel, variants=variants),
        grid=(hb,),
        in_specs=[pl.BlockSpec((1, n_dr, w, w), lambda h: (h, 0, 0, 0))],
        out_specs=pl.BlockSpec(out_blk, lambda h: (h, 0, 0, 0)),
        out_shape=jax.ShapeDtypeStruct((hb,) + out_blk[1:], F32),
        compiler_params=_params(1),
        name="natten_bias_assemble",
    )(toeplitz)


NATTEN_Q_ROWS = 2
NATTEN_BATCH = 4


def _natten_kernel(var_ref, ks_ref, q_ref, k_ref, v_ref, z_ref, bias_ref, qg_ref, kg_ref, o_ref,
                   qn_ref, kn_ref, sc_a, sc_b):
    s, hd = q_ref.shape[1:]
    w = GRID_W
    nq, nk = bias_ref.shape[2:]
    n_batches = s // (nq * NATTEN_BATCH)
    qn_ref[...] = (_rms(q_ref[0].astype(F32)) * qg_ref[...] * (float(hd) ** -0.5)).astype(BF16)
    kn_ref[...] = (_rms(k_ref[0].astype(F32)) * kg_ref[...]).astype(BF16)

    def rows_of(p):
        return (pl.ds(pl.multiple_of(p * nq, nq), nq), pl.ds(pl.multiple_of(ks_ref[p] * w, w), nk))

    def scores(p):
        q_rows, kv_rows = rows_of(p)
        qk = lax.dot_general(qn_ref[q_rows, :], kn_ref[kv_rows, :], (((1,), (1,)), ((), ())),
                             preferred_element_type=F32)
        return qk + bias_ref[0, var_ref[p]]

    def attend(p, sc):
        q_rows, kv_rows = rows_of(p)
        p_un = jnp.exp(sc - jnp.max(sc, axis=1, keepdims=True))
        o = jnp.dot(p_un.astype(BF16), v_ref[0, kv_rows, :], preferred_element_type=F32)
        o = o * (1.0 / jnp.sum(p_un, axis=1, keepdims=True))
        o_ref[0, q_rows, :] = (o * _silu(z_ref[0, q_rows, :].astype(F32))).astype(o_ref.dtype)

    def fill(buf, j):
        for u in range(NATTEN_BATCH):
            buf[u] = scores(j * NATTEN_BATCH + u)

    def drain(buf, j):
        for u in range(NATTEN_BATCH):
            attend(j * NATTEN_BATCH + u, buf[u])

    def body(k, carry):
        fill(sc_b, 2 * k + 1)
        drain(sc_a, 2 * k)
        fill(sc_a, 2 * k + 2)
        drain(sc_b, 2 * k + 1)
        return carry

    fill(sc_a, 0)
    lax.fori_loop(0, n_batches // 2 - 1, body, 0)
    fill(sc_b, n_batches - 1)
    drain(sc_a, n_batches - 2)
    drain(sc_b, n_batches - 1)


def _natten(proj, rpb, q_gain, k_gain, heads, hd, col0):
    b, s, _ = proj.shape
    win_r = (rpb.shape[1] + 1) // 2
    n_rows = s // GRID_W
    k_rows = win_r + NATTEN_Q_ROWS
    assert s % GRID_W == 0 and n_rows >= k_rows and col0 % hd == 0
    assert n_rows % (2 * NATTEN_Q_ROWS * NATTEN_BATCH) == 0
    variants, var_idx, k_start = _natten_plan(n_rows, win_r, NATTEN_Q_ROWS, k_rows)
    bias = _natten_bias(rpb, variants)
    sc_buf = pltpu.VMEM((NATTEN_BATCH,) + bias.shape[2:], F32)
    c0 = col0 // hd

    def col(part):
        return pl.BlockSpec((1, s, hd), lambda h, i: (i, 0, c0 + part * heads + h))

    smem = pl.BlockSpec(memory_space=pltpu.SMEM)
    vec = pl.BlockSpec((1, hd), lambda h, i: (0, 0))
    return pl.pallas_call(
        _natten_kernel,
        grid=(heads, b),
        in_specs=[smem, smem, col(0), col(1), col(2), col(3),
                  pl.BlockSpec((1,) + bias.shape[1:], lambda h, i: (h, 0, 0, 0)), vec, vec],
        out_specs=pl.BlockSpec((1, s, hd), lambda h, i: (i, 0, h)),
        out_shape=jax.ShapeDtypeStruct((b, s, heads * hd), BF16),
        scratch_shapes=[pltpu.VMEM((s, hd), BF16), pltpu.VMEM((s, hd), BF16), sc_buf, sc_buf],
        compiler_params=_params(2),
        name="neighbourhood_attention",
    )(jnp.asarray(var_idx, jnp.int32), jnp.asarray(k_start, jnp.int32), proj, proj, proj, proj, bias,
      q_gain.reshape(1, hd), k_gain.reshape(1, hd))


def _mm_out_kernel(a1_ref, a2_ref, w1_ref, w2_ref, x_ref, gate_ref, o_ref):
    acc = jnp.dot(a1_ref[0], w1_ref[...], preferred_element_type=F32)
    acc = acc + jnp.dot(a2_ref[0], w2_ref[...], preferred_element_type=F32)
    o_ref[0] = x_ref[0] + gate_ref[0] * acc


def _out_proj(a1, a1_blk, a2, a2_blk, w, x, gate, name):
    b, s, d = x.shape
    kh = w.shape[0] // 2
    tm, tn = _tile(s, 512), _tile(d, 512)
    return pl.pallas_call(
        _mm_out_kernel,
        grid=(b, s // tm, d // tn),
        in_specs=[pl.BlockSpec((1, tm, kh), lambda i, m, n: (i, m, a1_blk)),
                  pl.BlockSpec((1, tm, kh), lambda i, m, n: (i, m, a2_blk)),
                  pl.BlockSpec((kh, tn), lambda i, m, n: (0, n)),
                  pl.BlockSpec((kh, tn), lambda i, m, n: (1, n)),
                  pl.BlockSpec((1, tm, tn), lambda i, m, n: (i, m, n)),
                  pl.BlockSpec((1, 1, tn), lambda i, m, n: (i, 0, n))],
        out_specs=pl.BlockSpec((1, tm, tn), lambda i, m, n: (i, m, n)),
        out_shape=jax.ShapeDtypeStruct((b, s, d), F32),
        compiler_params=_params(3),
        name=name,
    )(a1, a2, w, w, x, gate.reshape(b, 1, d))


def _mm_glu_kernel(a_ref, wu_ref, wv_ref, wz_ref, uz_ref, gv_ref):
    a = a_ref[...]
    u = jnp.dot(a, wu_ref[...], preferred_element_type=F32)
    z = jnp.dot(a, wz_ref[...], preferred_element_type=F32)
    uz_ref[...] = (jax.nn.gelu(u) * _silu(z)).astype(BF16)
    v = jnp.dot(a, wv_ref[...], preferred_element_type=F32)
    gv_ref[...] = jax.nn.gelu(v).astype(BF16)


def _glu_proj(a, w, cw):
    m, k = a.shape
    tm, tn = _tile(m, 1024), _tile(cw, 256)
    nj = cw // tn
    out = jax.ShapeDtypeStruct((m, cw), BF16)
    ospec = pl.BlockSpec((tm, tn), lambda i, j: (i, j))
    return pl.pallas_call(
        _mm_glu_kernel,
        grid=(m // tm, nj),
        in_specs=[pl.BlockSpec((tm, k), lambda i, j: (i, 0)),
                  pl.BlockSpec((k, tn), lambda i, j: (0, j)),
                  pl.BlockSpec((k, tn), lambda i, j: (0, nj + j)),
                  pl.BlockSpec((k, tn), lambda i, j: (0, 2 * nj + j))],
        out_specs=[ospec, ospec],
        out_shape=[out, out],
        compiler_params=_params(2),
        name="gmlp_in_proj",
    )(a, w, w, w)


def _spatial_kernel(gv_ref, uz_ref, ws_ref, bs_ref, vg_ref, o_ref):
    groups = ws_ref.shape[0]
    cg = gv_ref.shape[2] // groups
    vn = (_rms(gv_ref[0].astype(F32)) * vg_ref[...]).astype(BF16)
    for g in range(groups):
        cols = slice(g * cg, (g + 1) * cg)
        sv = jnp.dot(ws_ref[g], vn[:, cols], preferred_element_type=F32) + bs_ref[:, g:g + 1]
        o_ref[0, :, cols] = (uz_ref[0, :, cols].astype(F32) * sv).astype(BF16)


def _spatial(gv, uz, ws, bs_t, v_gain):
    b, s, cw = gv.shape
    l = SPATIAL_CHUNK
    groups = ws.shape[0]
    assert s % l == 0 and ws.shape[1:] == (l, l)
    blk = pl.BlockSpec((1, l, cw), lambda i, n: (i, n, 0))
    return pl.pallas_call(
        _spatial_kernel,
        grid=(b, s // l),
        in_specs=[blk, blk,
                  pl.BlockSpec(ws.shape, lambda i, n: (0, 0, 0)),
                  pl.BlockSpec((l, groups), lambda i, n: (0, 0)),
                  pl.BlockSpec((1, cw), lambda i, n: (0, 0))],
        out_specs=blk,
        out_shape=jax.ShapeDtypeStruct((b, s, cw), BF16),
        compiler_params=_params(2),
        name="gmlp_spatial_gate",
    )(gv, uz, ws, bs_t, v_gain.reshape(1, cw))


def kernel(x, c, norm_g0, ada_w0, ada_b0, w_in0, a_conv_w, a_gate_b, a_norm_g, b_q_gain, b_k_gain, b_rpb, w_out0,
           norm_g1, ada_w1, ada_b1, w_in1, c_v_norm_g, c_w_s, c_b_s, w_out1):
    b, s, d = x.shape
    a_heads, dv = a_norm_g.shape
    a_width = a_heads * dv
    dk = dv // 2
    qk_width = a_heads * dk
    hd = b_q_gain.shape[0]
    b_heads, b_width = b_rpb.shape[0], b_rpb.shape[0] * hd
    n_gate = 4 * a_heads
    a_cols = 2 * qk_width + 3 * a_width
    assert w_in0.shape[1] == a_cols + n_gate + 4 * b_width
    cw = c_v_norm_g.shape[0]
    nc = s // MLSTM_CHUNK

    c_pad = jnp.pad(c, ((0, -b % 8), (0, 0)))
    mod0 = _modulation(c_pad, ada_w0, ada_b0)[:b]
    mod1 = _modulation(c_pad, ada_w1, ada_b1)[:b]
    shift0, scale0, gate0 = mod0[:, :d], mod0[:, d:2 * d], mod0[:, 2 * d:]
    shift1, scale1, gate1 = mod1[:, :d], mod1[:, d:2 * d], mod1[:, 2 * d:]

    h0 = _norm_mod(x, norm_g0, scale0, shift0).reshape(b * s, d)
    w_gate = jnp.pad(w_in0[:, a_cols:a_cols + n_gate], ((0, 0), (0, LANES - n_gate))).astype(BF16)
    proj = _matmul(h0, w_in0[:, :a_cols].astype(BF16), BF16, name="layer0_in_proj_a").reshape(b, s, a_cols)
    proj_b = _matmul(h0, w_in0[:, a_cols + n_gate:].astype(BF16), BF16,
                     name="layer0_in_proj_b").reshape(b, s, 4 * b_width)
    gates = _matmul(h0, w_gate, F32, name="layer0_gate_proj")[:, :n_gate]

    qk = _conv_silu(proj, a_conv_w, qk_width, dk)
    graw = gates.reshape(b, nc, MLSTM_CHUNK, n_gate).transpose(0, 1, 3, 2)
    ig, pre, suf = _gate_prep(graw, a_gate_b)
    hh = a_heads
    zeros = jnp.zeros_like(ig[:, :, :hh])
    pack = jnp.stack([ig[:, :, :hh], pre[:, :, hh:2 * hh], ig[:, :, 2 * hh:3 * hh], suf[:, :, 3 * hh:],
                      zeros, zeros, zeros, zeros], axis=3)
    pack = pack.transpose(0, 2, 1, 3, 4)
    hn = _mlstm(qk, proj, pack, a_norm_g, a_heads, dk, dv, 2 * qk_width)
    y_a = _gate_a(hn, proj, 2 * qk_width + a_width, 2 * qk_width + 2 * a_width)

    y_b = _natten(proj_b, b_rpb, b_q_gain, b_k_gain, b_heads, hd, 0)

    assert a_width == b_width
    x1 = _out_proj(y_a, 0, y_b, 0, w_out0.astype(BF16), x, gate0, "layer0_out_proj")

    h1 = _norm_mod(x1, norm_g1, scale1, shift1).reshape(b * s, d)
    uz, gv = _glu_proj(h1, w_in1.astype(BF16), cw)
    y = _spatial(gv.reshape(b, s, cw), uz.reshape(b, s, cw), c_w_s.astype(BF16), c_b_s.T, c_v_norm_g)
    return _out_proj(y, 0, y, 1, w_out1.astype(BF16), x1, gate1, "layer1_out_proj")
```

```python
import functools

import jax
import jax.numpy as jnp
from jax import lax
from jax.experimental import pallas as pl
from jax.experimental.pallas import tpu as pltpu

F32 = jnp.float32
BF16 = jnp.bfloat16
HIGHEST = lax.Precision.HIGHEST

EPS = 1e-6
GRID_W = 64
MLSTM_CHUNK = 128
SPATIAL_CHUNK = 128
LANES = 128
V7X_VMEM_LIMIT_BYTES = 52 * 1024 * 1024


def _params(n_axes, vmem=V7X_VMEM_LIMIT_BYTES):
    return pltpu.CompilerParams(dimension_semantics=("arbitrary",) * n_axes, vmem_limit_bytes=vmem)


def _tile(dim, pref):
    t = min(dim, pref)
    assert dim % t == 0, (dim, pref)
    return t


def _silu(x):
    return x * jax.nn.sigmoid(x)


def _rms(x):
    return x * lax.rsqrt(jnp.mean(x * x, axis=-1, keepdims=True) + EPS)


def _ada_kernel(c_ref, w_ref, b_ref, o_ref):
    a = _silu(c_ref[...]).astype(BF16)
    o_ref[...] = jnp.dot(a, w_ref[...].astype(BF16), preferred_element_type=F32) + b_ref[...]


def _modulation(c_pad, w, b):
    rows, d = c_pad.shape
    n = w.shape[1]
    tn = _tile(n, 512)
    return pl.pallas_call(
        _ada_kernel,
        grid=(n // tn,),
        in_specs=[pl.BlockSpec((rows, d), lambda j: (0, 0)),
                  pl.BlockSpec((d, tn), lambda j: (0, j)),
                  pl.BlockSpec((1, tn), lambda j: (0, j))],
        out_specs=pl.BlockSpec((rows, tn), lambda j: (0, j)),
        out_shape=jax.ShapeDtypeStruct((rows, n), F32),
        compiler_params=_params(1),
        name="ada_modulation",
    )(c_pad, w, b.reshape(1, n))


def _norm_mod_kernel(x_ref, g_ref, sc_ref, sh_ref, o_ref):
    y = _rms(x_ref[0]) * g_ref[...]
    o_ref[0] = (y * (1.0 + sc_ref[0]) + sh_ref[0]).astype(BF16)


def _norm_mod(x, g, scale, shift):
    b, s, d = x.shape
    ts = _tile(s, 256)
    vec = pl.BlockSpec((1, 1, d), lambda i, j: (i, 0, 0))
    return pl.pallas_call(
        _norm_mod_kernel,
        grid=(b, s // ts),
        in_specs=[pl.BlockSpec((1, ts, d), lambda i, j: (i, j, 0)),
                  pl.BlockSpec((1, d), lambda i, j: (0, 0)), vec, vec],
        out_specs=pl.BlockSpec((1, ts, d), lambda i, j: (i, j, 0)),
        out_shape=jax.ShapeDtypeStruct((b, s, d), BF16),
        compiler_params=_params(2),
        name="norm_modulate",
    )(x, g.reshape(1, d), scale.reshape(b, 1, d), shift.reshape(b, 1, d))


def _mm_kernel(a_ref, w_ref, o_ref):
    o_ref[...] = jnp.dot(a_ref[...], w_ref[...], preferred_element_type=F32).astype(o_ref.dtype)


def _matmul(a, w, out_dtype, tm_pref=1024, tn_pref=1024, name="matmul"):
    m, k = a.shape
    n = w.shape[1]
    tm, tn = _tile(m, tm_pref), _tile(n, tn_pref)
    return pl.pallas_call(
        _mm_kernel,
        grid=(m // tm, n // tn),
        in_specs=[pl.BlockSpec((tm, k), lambda i, j: (i, 0)),
                  pl.BlockSpec((k, tn), lambda i, j: (0, j))],
        out_specs=pl.BlockSpec((tm, tn), lambda i, j: (i, j)),
        out_shape=jax.ShapeDtypeStruct((m, n), out_dtype),
        compiler_params=_params(2),
        name=name,
    )(a, w)


def _mm_og_kernel(a_ref, wo_ref, wz_ref, o_ref):
    a = a_ref[...]
    o = jnp.dot(a, wo_ref[...], preferred_element_type=F32)
    z = jnp.dot(a, wz_ref[...], preferred_element_type=F32)
    o_ref[...] = (jax.nn.sigmoid(o) * _silu(z)).astype(BF16)


def _og_proj(a, w):
    m, k = a.shape
    n = w.shape[1] // 2
    tm, tn = _tile(m, 1024), _tile(n, 512)
    nj = n // tn
    return pl.pallas_call(
        _mm_og_kernel,
        grid=(m // tm, nj),
        in_specs=[pl.BlockSpec((tm, k), lambda i, j: (i, 0)),
                  pl.BlockSpec((k, tn), lambda i, j: (0, j)),
                  pl.BlockSpec((k, tn), lambda i, j: (0, nj + j))],
        out_specs=pl.BlockSpec((tm, tn), lambda i, j: (i, j)),
        out_shape=jax.ShapeDtypeStruct((m, n), BF16),
        compiler_params=_params(2),
        name="layer0_og_proj",
    )(a, w, w)


def _shift_cast_kernel(x0_ref, x1_ref, o_ref, *, shift):
    tn = o_ref.shape[1]
    x = jnp.concatenate([x0_ref[...], x1_ref[...]], axis=1)
    o_ref[...] = x[:, shift:shift + tn].astype(o_ref.dtype)


def _cast_columns(w, start, width, dtype=BF16):
    k = w.shape[0]
    tk, tn = _tile(k, 512), _tile(width, 512)
    base, shift = divmod(start, tn)
    assert 0 < shift <= LANES and (base + width // tn) * tn + shift <= w.shape[1]
    per = tn // LANES
    return pl.pallas_call(
        functools.partial(_shift_cast_kernel, shift=shift),
        grid=(k // tk, width // tn),
        in_specs=[pl.BlockSpec((tk, tn), lambda i, j: (i, base + j)),
                  pl.BlockSpec((tk, LANES), lambda i, j: (i, (base + j + 1) * per))],
        out_specs=pl.BlockSpec((tk, tn), lambda i, j: (i, j)),
        out_shape=jax.ShapeDtypeStruct((k, width), dtype),
        compiler_params=_params(2),
        name="cast_shifted_columns",
    )(w, w)


def _conv3_silu(x, w):
    s = x.shape[0]
    row = lax.broadcasted_iota(jnp.int32, x.shape, 0)
    x_prev = jnp.where(row == 0, 0.0, pltpu.roll(x, 1, axis=0))
    x_next = jnp.where(row == s - 1, 0.0, pltpu.roll(x, s - 1, axis=0))
    return _silu(x_prev * w[0:1] + x * w[1:2] + x_next * w[2:3])


def _conv_kernel(xq_ref, xk_ref, wq_ref, wk_ref, q_ref, k_ref, kt_ref, *, q_scale):
    q_ref[0] = (_conv3_silu(xq_ref[0].astype(F32), wq_ref[...]) * q_scale).astype(BF16)
    k = _conv3_silu(xk_ref[0].astype(F32), wk_ref[...])
    k_ref[0] = k.astype(BF16)
    l = kt_ref.shape[3]
    for c in range(kt_ref.shape[1]):
        kt_ref[0, c] = k[c * l:(c + 1) * l, :].T.astype(BF16)


def _conv_silu(proj, conv_w, qk_width, dk):
    b, s, _ = proj.shape
    assert conv_w.shape[0] == 3, "centred depthwise conv is written for 3 taps"
    tc = _tile(qk_width, 256)
    nq = qk_width // tc
    l = MLSTM_CHUNK
    kern = functools.partial(_conv_kernel, q_scale=float(dk) ** -0.5)
    col = pl.BlockSpec((1, s, tc), lambda i, j: (i, 0, j))
    return pl.pallas_call(
        kern,
        grid=(b, nq),
        in_specs=[col, pl.BlockSpec((1, s, tc), lambda i, j: (i, 0, nq + j)),
                  pl.BlockSpec((3, tc), lambda i, j: (0, j)), pl.BlockSpec((3, tc), lambda i, j: (0, nq + j))],
        out_specs=[col, col, pl.BlockSpec((1, s // l, tc, l), lambda i, j: (i, 0, j, 0))],
        out_shape=[jax.ShapeDtypeStruct((b, s, qk_width), BF16), jax.ShapeDtypeStruct((b, s, qk_width), BF16),
                   jax.ShapeDtypeStruct((b, s // l, qk_width, l), BF16)],
        compiler_params=_params(2),
        name="qk_conv_silu",
    )(proj, proj, conv_w, conv_w)


def _gates_kernel(x_ref, b_ref, ig_ref, pre_ref, suf_ref):
    nc, r, l = x_ref.shape[1:]
    x = x_ref[0] + b_ref[...][None]
    ig_ref[0] = x
    lf = (jnp.minimum(x, 0.0) - jnp.log1p(jnp.exp(-jnp.abs(x)))).reshape(nc * r, l)
    u = lax.broadcasted_iota(jnp.int32, (l, l), 0)
    t = lax.broadcasted_iota(jnp.int32, (l, l), 1)
    pre = jnp.dot(lf, (u <= t).astype(F32), precision=HIGHEST, preferred_element_type=F32)
    suf = jnp.dot(lf, (u >= t).astype(F32), precision=HIGHEST, preferred_element_type=F32)
    pre_ref[0] = pre.reshape(nc, r, l)
    suf_ref[0] = suf.reshape(nc, r, l)


def _gate_prep(graw, bias):
    b, nc, r, l = graw.shape
    spec = pl.BlockSpec((1, nc, r, l), lambda i: (i, 0, 0, 0))
    shape = jax.ShapeDtypeStruct(graw.shape, F32)
    return pl.pallas_call(
        _gates_kernel,
        grid=(b,),
        in_specs=[spec, pl.BlockSpec((r, 1), lambda i: (0, 0))],
        out_specs=[spec, spec, spec],
        out_shape=[shape, shape, shape],
        compiler_params=_params(1),
        name="mlstm_gate_prep",
    )(graw, bias.reshape(r, 1))


def _row_to_col(row_vec, eye):
    return jnp.sum(jnp.where(eye, row_vec, 0.0), axis=1, keepdims=True)


def _mlstm_chunk(q, k, kt, v, ig, g, tot, m, n, c_ref, *, causal):
    l = q.shape[0]
    r_i = lax.broadcasted_iota(jnp.int32, (l, l), 0)
    c_i = lax.broadcasted_iota(jnp.int32, (l, l), 1)
    eye = r_i == c_i
    g_col = _row_to_col(g, eye)
    d = g_col - g + ig
    d = jnp.where((c_i <= r_i) if causal else (c_i >= r_i), d, -jnp.inf)
    gm = g_col + m
    m_t = jnp.maximum(gm, jnp.max(d, axis=1, keepdims=True))
    p = jnp.exp(d - m_t)
    inter = jnp.exp(gm - m_t)
    s_ = jnp.dot(q, kt, preferred_element_type=F32) * p
    qc = jnp.dot(q, c_ref[...].astype(BF16), preferred_element_type=F32)
    num = jnp.dot(s_.astype(BF16), v, preferred_element_type=F32) + inter * qc
    qn = jnp.sum(q.astype(F32) * n, axis=1, keepdims=True)
    den = jnp.sum(s_, axis=1, keepdims=True) + inter * qn
    h = num * (1.0 / jnp.maximum(jnp.abs(den), jnp.exp(-m_t)))
    ds = tot - g + ig
    m_new = jnp.maximum(tot + m, jnp.max(ds, axis=1, keepdims=True))
    w = jnp.exp(ds - m_new)
    decay = jnp.exp(tot + m - m_new)
    wkt = (kt.astype(F32) * w).astype(BF16)
    c_ref[...] = decay * c_ref[...] + jnp.dot(wkt, v, preferred_element_type=F32)
    w_rows = jnp.broadcast_to(w, (8, l)).astype(BF16)
    n_new = decay * n + jnp.dot(w_rows, k, preferred_element_type=F32)[0:1]
    return h, m_new, n_new


MLSTM_UNROLL = 2


def _mlstm_kernel(q_ref, k_ref, kt_ref, v_ref, og_ref, gp_ref, ng_ref, o_ref, acc_ref, cf_ref, cb_ref):
    s, dv = v_ref.shape[1:]
    dk = q_ref.shape[2]
    l = MLSTM_CHUNK
    nc = s // l
    cf_ref[...] = jnp.zeros_like(cf_ref)
    cb_ref[...] = jnp.zeros_like(cb_ref)
    gain = ng_ref[0]

    def direction(c, state, c_ref, causal):
        m, n = state
        rows = pl.ds(pl.multiple_of(c * l, l), l)
        q, k, v = q_ref[0, rows, :], k_ref[0, rows, :], v_ref[0, rows, :]
        gp = gp_ref[0, 0, c]
        if causal:
            ig, g = gp[0:1], gp[1:2]
            tot = g[:, l - 1:l]
        else:
            ig, g = gp[2:3], gp[3:4]
            tot = g[:, 0:1]
        h, m_new, n_new = _mlstm_chunk(q, k, kt_ref[0, c], v, ig, g, tot, m, n, c_ref, causal=causal)
        return rows, h, (m_new, n_new)

    def emit(rows, h, first_touch):
        if first_touch:
            acc_ref[rows, :] = h
        else:
            y = _rms(acc_ref[rows, :] + h) * gain * og_ref[0, rows, :].astype(F32)
            o_ref[0, rows, :] = y.astype(o_ref.dtype)

    def body(first_touch, i, carry):
        st_f, st_b = carry
        for u in range(MLSTM_UNROLL):
            c = i * MLSTM_UNROLL + u
            rows_f, h_f, st_f = direction(c, st_f, cf_ref, True)
            emit(rows_f, h_f, first_touch)
            rows_b, h_b, st_b = direction(nc - 1 - c, st_b, cb_ref, False)
            emit(rows_b, h_b, first_touch)
        return st_f, st_b

    init = (jnp.zeros((1, 1), F32), jnp.zeros((1, dk), F32))
    n_it = nc // MLSTM_UNROLL
    carry = lax.fori_loop(0, n_it // 2, functools.partial(body, True), (init, init))
    lax.fori_loop(n_it // 2, n_it, functools.partial(body, False), carry)


def _mlstm(q, k, kt, qkv, og, gate_pack, norm_g, heads, dk, dv, v_col0):
    b, s, _ = qkv.shape
    assert s % (2 * MLSTM_UNROLL * MLSTM_CHUNK) == 0 and v_col0 % dv == 0
    nc = s // MLSTM_CHUNK
    seq_dk = pl.BlockSpec((1, s, dk), lambda i, h: (i, 0, h))
    seq_dv = pl.BlockSpec((1, s, dv), lambda i, h: (i, 0, h))
    return pl.pallas_call(
        _mlstm_kernel,
        grid=(b, heads),
        in_specs=[seq_dk, seq_dk,
                  pl.BlockSpec((1, nc, dk, MLSTM_CHUNK), lambda i, h: (i, 0, h, 0)),
                  pl.BlockSpec((1, s, dv), lambda i, h: (i, 0, v_col0 // dv + h)),
                  seq_dv,
                  pl.BlockSpec((1, 1, nc, 8, MLSTM_CHUNK), lambda i, h: (i, h, 0, 0, 0)),
                  pl.BlockSpec((1, 1, dv), lambda i, h: (h, 0, 0))],
        out_specs=seq_dv,
        out_shape=jax.ShapeDtypeStruct((b, s, heads * dv), BF16),
        scratch_shapes=[pltpu.VMEM((s, dv), F32),
                        pltpu.VMEM((dk, dv), F32),
                        pltpu.VMEM((dk, dv), F32)],
        compiler_params=_params(2),
        name="mlstm_bidirectional",
    )(q, k, kt, qkv, og, gate_pack, norm_g.reshape(heads, 1, dv))


def _rpb_toeplitz_kernel(r_ref, o_ref, *, win_c):
    dpad, n = r_ref.shape[1], o_ref.shape[1]
    dd = lax.broadcasted_iota(jnp.int32, (dpad, n), 0)
    cc = lax.broadcasted_iota(jnp.int32, (dpad, n), 1)
    shift = GRID_W.bit_length() - 1
    qc, kc = cc >> shift, cc & (GRID_W - 1)
    dc = jnp.clip(kc - qc + (win_c - 1), 0, 2 * win_c - 2)
    onehot = jnp.where(dd == dc, 1.0, 0.0).astype(F32)
    vals = jnp.dot(r_ref[...], onehot, precision=HIGHEST, preferred_element_type=F32)
    co = lax.broadcasted_iota(jnp.int32, (1, n), 1)
    qo, ko = co >> shift, co & (GRID_W - 1)
    cs = jnp.clip(qo - win_c // 2, 0, GRID_W - win_c)
    o_ref[...] = jnp.where((ko >= cs) & (ko < cs + win_c), vals, -jnp.inf)


def _natten_plan(n_rows, win_r, q_rows, k_rows):
    variants, var_idx, k_start = [], [], []
    for r0 in range(0, n_rows, q_rows):
        kp = min(max(r0 - win_r // 2, 0), n_rows - k_rows)
        pattern = []
        for i in range(q_rows):
            r = r0 + i
            rs = min(max(r - win_r // 2, 0), n_rows - win_r)
            assert kp <= rs and rs + win_r <= kp + k_rows
            pattern.append(tuple((kp + jj - r + win_r - 1) if rs <= kp + jj < rs + win_r else None
                                 for jj in range(k_rows)))
        pattern = tuple(pattern)
        if pattern not in variants:
            variants.append(pattern)
        var_idx.append(variants.index(pattern))
        k_start.append(kp)
    return variants, var_idx, k_start


def _bias_assemble_kernel(t_ref, o_ref, *, variants):
    w = GRID_W
    masked = jnp.full((w, w), -jnp.inf, F32)
    for v, pattern in enumerate(variants):
        for i, row in enumerate(pattern):
            tiles = [masked if dr is None else t_ref[0, dr] for dr in row]
            o_ref[0, v, i * w:(i + 1) * w, :] = jnp.concatenate(tiles, axis=1)


def _natten_bias(rpb, variants):
    hb, n_dr, n_dc = rpb.shape
    win_c = (n_dc + 1) // 2
    w = GRID_W
    dpad = 32
    assert n_dc <= dpad and (hb * n_dr) % 8 == 0
    r = jnp.pad(rpb, ((0, 0), (0, 0), (0, dpad - n_dc))).reshape(hb * n_dr, dpad)
    toeplitz = pl.pallas_call(
        functools.partial(_rpb_toeplitz_kernel, win_c=win_c),
        grid=(1,),
        in_specs=[pl.BlockSpec(r.shape, lambda i: (0, 0))],
        out_specs=pl.BlockSpec((r.shape[0], w * w), lambda i: (0, 0)),
        out_shape=jax.ShapeDtypeStruct((r.shape[0], w * w), F32),
        compiler_params=_params(1),
        name="natten_rpb_toeplitz",
    )(r).reshape(hb, n_dr, w, w)
    q_rows, k_rows = len(variants[0]), len(variants[0][0])
    out_blk = (1, len(variants), q_rows * w, k_rows * w)
    return pl.pallas_call(
        functools.partial(_bias_assemble_kernel, variants=variants),
        grid=(hb,),
        in_specs=[pl.BlockSpec((1, n_dr, w, w), lambda h: (h, 0, 0, 0))],
        out_specs=pl.BlockSpec(out_blk, lambda h: (h, 0, 0, 0)),
        out_shape=jax.ShapeDtypeStruct((hb,) + out_blk[1:], F32),
        compiler_params=_params(1),
        name="natten_bias_assemble",
    )(toeplitz)


NATTEN_Q_ROWS = 2
NATTEN_BATCH = 4


def _natten_kernel(var_ref, ks_ref, q_ref, k_ref, v_ref, z_ref, bias_ref, qg_ref, kg_ref, o_ref,
                   qn_ref, kn_ref, sc_a, sc_b):
    s, hd = q_ref.shape[1:]
    w = GRID_W
    nq, nk = bias_ref.shape[2:]
    n_batches = s // (nq * NATTEN_BATCH)
    qn_ref[...] = (_rms(q_ref[0].astype(F32)) * qg_ref[...] * (float(hd) ** -0.5)).astype(BF16)
    kn_ref[...] = (_rms(k_ref[0].astype(F32)) * kg_ref[...]).astype(BF16)

    def rows_of(p):
        return (pl.ds(pl.multiple_of(p * nq, nq), nq), pl.ds(pl.multiple_of(ks_ref[p] * w, w), nk))

    def scores(p):
        q_rows, kv_rows = rows_of(p)
        qk = lax.dot_general(qn_ref[q_rows, :], kn_ref[kv_rows, :], (((1,), (1,)), ((), ())),
                             preferred_element_type=F32)
        return qk + bias_ref[0, var_ref[p]]

    def attend(p, sc):
        q_rows, kv_rows = rows_of(p)
        p_un = jnp.exp(sc - jnp.max(sc, axis=1, keepdims=True))
        o = jnp.dot(p_un.astype(BF16), v_ref[0, kv_rows, :], preferred_element_type=F32)
        o = o * (1.0 / jnp.sum(p_un, axis=1, keepdims=True))
        o_ref[0, q_rows, :] = (o * _silu(z_ref[0, q_rows, :].astype(F32))).astype(o_ref.dtype)

    def fill(buf, j):
        for u in range(NATTEN_BATCH):
            buf[u] = scores(j * NATTEN_BATCH + u)

    def drain(buf, j):
        for u in range(NATTEN_BATCH):
            attend(j * NATTEN_BATCH + u, buf[u])

    def body(k, carry):
        fill(sc_b, 2 * k + 1)
        drain(sc_a, 2 * k)
        fill(sc_a, 2 * k + 2)
        drain(sc_b, 2 * k + 1)
        return carry

    fill(sc_a, 0)
    lax.fori_loop(0, n_batches // 2 - 1, body, 0)
    fill(sc_b, n_batches - 1)
    drain(sc_a, n_batches - 2)
    drain(sc_b, n_batches - 1)


def _natten(proj, rpb, q_gain, k_gain, heads, hd, col0):
    b, s, _ = proj.shape
    win_r = (rpb.shape[1] + 1) // 2
    n_rows = s // GRID_W
    k_rows = win_r + NATTEN_Q_ROWS
    assert s % GRID_W == 0 and n_rows >= k_rows and col0 % hd == 0
    assert n_rows % (2 * NATTEN_Q_ROWS * NATTEN_BATCH) == 0
    variants, var_idx, k_start = _natten_plan(n_rows, win_r, NATTEN_Q_ROWS, k_rows)
    bias = _natten_bias(rpb, variants)
    sc_buf = pltpu.VMEM((NATTEN_BATCH,) + bias.shape[2:], F32)
    c0 = col0 // hd

    def col(part):
        return pl.BlockSpec((1, s, hd), lambda h, i: (i, 0, c0 + part * heads + h))

    smem = pl.BlockSpec(memory_space=pltpu.SMEM)
    vec = pl.BlockSpec((1, hd), lambda h, i: (0, 0))
    return pl.pallas_call(
        _natten_kernel,
        grid=(heads, b),
        in_specs=[smem, smem, col(0), col(1), col(2), col(3),
                  pl.BlockSpec((1,) + bias.shape[1:], lambda h, i: (h, 0, 0, 0)), vec, vec],
        out_specs=pl.BlockSpec((1, s, hd), lambda h, i: (i, 0, h)),
        out_shape=jax.ShapeDtypeStruct((b, s, heads * hd), BF16),
        scratch_shapes=[pltpu.VMEM((s, hd), BF16), pltpu.VMEM((s, hd), BF16), sc_buf, sc_buf],
        compiler_params=_params(2),
        name="neighbourhood_attention",
    )(jnp.asarray(var_idx, jnp.int32), jnp.asarray(k_start, jnp.int32), proj, proj, proj, proj, bias,
      q_gain.reshape(1, hd), k_gain.reshape(1, hd))


def _mm_out_kernel(a1_ref, a2_ref, w1_ref, w2_ref, x_ref, gate_ref, o_ref):
    acc = jnp.dot(a1_ref[0], w1_ref[...], preferred_element_type=F32)
    acc = acc + jnp.dot(a2_ref[0], w2_ref[...], preferred_element_type=F32)
    o_ref[0] = x_ref[0] + gate_ref[0] * acc


def _out_proj(a1, a1_blk, a2, a2_blk, w, x, gate, name):
    b, s, d = x.shape
    kh = w.shape[0] // 2
    tm, tn = _tile(s, 512), _tile(d, 512)
    return pl.pallas_call(
        _mm_out_kernel,
        grid=(b, s // tm, d // tn),
        in_specs=[pl.BlockSpec((1, tm, kh), lambda i, m, n: (i, m, a1_blk)),
                  pl.BlockSpec((1, tm, kh), lambda i, m, n: (i, m, a2_blk)),
                  pl.BlockSpec((kh, tn), lambda i, m, n: (0, n)),
                  pl.BlockSpec((kh, tn), lambda i, m, n: (1, n)),
                  pl.BlockSpec((1, tm, tn), lambda i, m, n: (i, m, n)),
                  pl.BlockSpec((1, 1, tn), lambda i, m, n: (i, 0, n))],
        out_specs=pl.BlockSpec((1, tm, tn), lambda i, m, n: (i, m, n)),
        out_shape=jax.ShapeDtypeStruct((b, s, d), F32),
        compiler_params=_params(3),
        name=name,
    )(a1, a2, w, w, x, gate.reshape(b, 1, d))


def _mm_glu_kernel(a_ref, wu_ref, wv_ref, wz_ref, uz_ref, gv_ref):
    a = a_ref[...]
    u = jnp.dot(a, wu_ref[...], preferred_element_type=F32)
    z = jnp.dot(a, wz_ref[...], preferred_element_type=F32)
    uz_ref[...] = (jax.nn.gelu(u) * _silu(z)).astype(BF16)
    v = jnp.dot(a, wv_ref[...], preferred_element_type=F32)
    gv_ref[...] = jax.nn.gelu(v).astype(BF16)


def _glu_proj(a, w, cw):
    m, k = a.shape
    tm, tn = _tile(m, 1024), _tile(cw, 256)
    nj = cw // tn
    out = jax.ShapeDtypeStruct((m, cw), BF16)
    ospec = pl.BlockSpec((tm, tn), lambda i, j: (i, j))
    return pl.pallas_call(
        _mm_glu_kernel,
        grid=(m // tm, nj),
        in_specs=[pl.BlockSpec((tm, k), lambda i, j: (i, 0)),
                  pl.BlockSpec((k, tn), lambda i, j: (0, j)),
                  pl.BlockSpec((k, tn), lambda i, j: (0, nj + j)),
                  pl.BlockSpec((k, tn), lambda i, j: (0, 2 * nj + j))],
        out_specs=[ospec, ospec],
        out_shape=[out, out],
        compiler_params=_params(2),
        name="gmlp_in_proj",
    )(a, w, w, w)


def _spatial_kernel(gv_ref, uz_ref, ws_ref, bs_ref, vg_ref, o_ref):
    groups = ws_ref.shape[0]
    cg = gv_ref.shape[2] // groups
    vn = (_rms(gv_ref[0].astype(F32)) * vg_ref[...]).astype(BF16)
    for g in range(groups):
        cols = slice(g * cg, (g + 1) * cg)
        sv = jnp.dot(ws_ref[g], vn[:, cols], preferred_element_type=F32) + bs_ref[:, g:g + 1]
        o_ref[0, :, cols] = (uz_ref[0, :, cols].astype(F32) * sv).astype(BF16)


def _spatial(gv, uz, ws, bs_t, v_gain):
    b, s, cw = gv.shape
    l = SPATIAL_CHUNK
    groups = ws.shape[0]
    assert s % l == 0 and ws.shape[1:] == (l, l)
    blk = pl.BlockSpec((1, l, cw), lambda i, n: (i, n, 0))
    return pl.pallas_call(
        _spatial_kernel,
        grid=(b, s // l),
        in_specs=[blk, blk,
                  pl.BlockSpec(ws.shape, lambda i, n: (0, 0, 0)),
                  pl.BlockSpec((l, groups), lambda i, n: (0, 0)),
                  pl.BlockSpec((1, cw), lambda i, n: (0, 0))],
        out_specs=blk,
        out_shape=jax.ShapeDtypeStruct((b, s, cw), BF16),
        compiler_params=_params(2),
        name="gmlp_spatial_gate",
    )(gv, uz, ws, bs_t, v_gain.reshape(1, cw))


def kernel(x, c, norm_g0, ada_w0, ada_b0, w_in0, a_conv_w, a_gate_b, a_norm_g, b_q_gain, b_k_gain, b_rpb, w_out0,
           norm_g1, ada_w1, ada_b1, w_in1, c_v_norm_g, c_w_s, c_b_s, w_out1):
    b, s, d = x.shape
    a_heads, dv = a_norm_g.shape
    a_width = a_heads * dv
    dk = dv // 2
    qk_width = a_heads * dk
    hd = b_q_gain.shape[0]
    b_heads, b_width = b_rpb.shape[0], b_rpb.shape[0] * hd
    n_gate = 4 * a_heads
    a_cols = 2 * qk_width + 3 * a_width
    assert w_in0.shape[1] == a_cols + n_gate + 4 * b_width
    cw = c_v_norm_g.shape[0]
    nc = s // MLSTM_CHUNK

    c_pad = jnp.pad(c, ((0, -b % 8), (0, 0)))
    mod0 = _modulation(c_pad, ada_w0, ada_b0)[:b]
    mod1 = _modulation(c_pad, ada_w1, ada_b1)[:b]
    shift0, scale0, gate0 = mod0[:, :d], mod0[:, d:2 * d], mod0[:, 2 * d:]
    shift1, scale1, gate1 = mod1[:, :d], mod1[:, d:2 * d], mod1[:, 2 * d:]

    h0 = _norm_mod(x, norm_g0, scale0, shift0).reshape(b * s, d)
    qkv_cols = 2 * qk_width + a_width
    w_gate = jnp.pad(w_in0[:, a_cols:a_cols + n_gate], ((0, 0), (0, LANES - n_gate))).astype(BF16)
    qkv = _matmul(h0, w_in0[:, :qkv_cols].astype(BF16), BF16, name="layer0_in_proj_qkv").reshape(b, s, qkv_cols)
    og = _og_proj(h0, w_in0[:, qkv_cols:a_cols].astype(BF16)).reshape(b, s, a_width)
    proj_b = _matmul(h0, _cast_columns(w_in0, a_cols + n_gate, 4 * b_width), BF16,
                     name="layer0_in_proj_b").reshape(b, s, 4 * b_width)
    gates = _matmul(h0, w_gate, F32, name="layer0_gate_proj")[:, :n_gate]

    q, k, kt = _conv_silu(qkv, a_conv_w, qk_width, dk)
    graw = gates.reshape(b, nc, MLSTM_CHUNK, n_gate).transpose(0, 1, 3, 2)
    ig, pre, suf = _gate_prep(graw, a_gate_b)
    hh = a_heads
    zeros = jnp.zeros_like(ig[:, :, :hh])
    pack = jnp.stack([ig[:, :, :hh], pre[:, :, hh:2 * hh], ig[:, :, 2 * hh:3 * hh], suf[:, :, 3 * hh:],
                      zeros, zeros, zeros, zeros], axis=3)
    pack = pack.transpose(0, 2, 1, 3, 4)
    y_a = _mlstm(q, k, kt, qkv, og, pack, a_norm_g, a_heads, dk, dv, 2 * qk_width)

    y_b = _natten(proj_b, b_rpb, b_q_gain, b_k_gain, b_heads, hd, 0)

    assert a_width == b_width
    x1 = _out_proj(y_a, 0, y_b, 0, w_out0.astype(BF16), x, gate0, "layer0_out_proj")

    h1 = _norm_mod(x1, norm_g1, scale1, shift1).reshape(b * s, d)
    uz, gv = _glu_proj(h1, w_in1.astype(BF16), cw)
    y = _spatial(gv.reshape(b, s, cw), uz.reshape(b, s, cw), c_w_s.astype(BF16), c_b_s.T, c_v_norm_g)
    return _out_proj(y, 0, y, 1, w_out1.astype(BF16), x1, gate1, "layer1_out_proj")
```

```python
import functools

import jax
import jax.numpy as jnp
from jax import lax
from jax.experimental import pallas as pl
from jax.experimental.pallas import tpu as pltpu

F32 = jnp.float32
BF16 = jnp.bfloat16
HIGHEST = lax.Precision.HIGHEST

EPS = 1e-6
GRID_W = 64
MLSTM_CHUNK = 128
SPATIAL_CHUNK = 128
LANES = 128
V7X_VMEM_LIMIT_BYTES = 52 * 1024 * 1024


def _params(n_axes, vmem=V7X_VMEM_LIMIT_BYTES):
    return pltpu.CompilerParams(dimension_semantics=("arbitrary",) * n_axes, vmem_limit_bytes=vmem)


def _tile(dim, pref):
    t = min(dim, pref)
    assert dim % t == 0, (dim, pref)
    return t


def _silu(x):
    return x * jax.nn.sigmoid(x)


def _rms(x):
    return x * lax.rsqrt(jnp.mean(x * x, axis=-1, keepdims=True) + EPS)


def _ada_kernel(c_ref, w_ref, b_ref, o_ref):
    a = _silu(c_ref[...]).astype(BF16)
    o_ref[...] = jnp.dot(a, w_ref[...].astype(BF16), preferred_element_type=F32) + b_ref[...]


def _modulation(c_pad, w, b):
    rows, d = c_pad.shape
    n = w.shape[1]
    tn = _tile(n, 512)
    return pl.pallas_call(
        _ada_kernel,
        grid=(n // tn,),
        in_specs=[pl.BlockSpec((rows, d), lambda j: (0, 0)),
                  pl.BlockSpec((d, tn), lambda j: (0, j)),
                  pl.BlockSpec((1, tn), lambda j: (0, j))],
        out_specs=pl.BlockSpec((rows, tn), lambda j: (0, j)),
        out_shape=jax.ShapeDtypeStruct((rows, n), F32),
        compiler_params=_params(1),
        name="ada_modulation",
    )(c_pad, w, b.reshape(1, n))


def _norm_mod_kernel(x_ref, g_ref, sc_ref, sh_ref, o_ref):
    y = _rms(x_ref[0]) * g_ref[...]
    o_ref[0] = (y * (1.0 + sc_ref[0]) + sh_ref[0]).astype(BF16)


def _norm_mod(x, g, scale, shift):
    b, s, d = x.shape
    ts = _tile(s, 256)
    vec = pl.BlockSpec((1, 1, d), lambda i, j: (i, 0, 0))
    return pl.pallas_call(
        _norm_mod_kernel,
        grid=(b, s // ts),
        in_specs=[pl.BlockSpec((1, ts, d), lambda i, j: (i, j, 0)),
                  pl.BlockSpec((1, d), lambda i, j: (0, 0)), vec, vec],
        out_specs=pl.BlockSpec((1, ts, d), lambda i, j: (i, j, 0)),
        out_shape=jax.ShapeDtypeStruct((b, s, d), BF16),
        compiler_params=_params(2),
        name="norm_modulate",
    )(x, g.reshape(1, d), scale.reshape(b, 1, d), shift.reshape(b, 1, d))


def _proj_kernel(a_ref, *refs, n_w, w_transposed, epilogue):
    w_refs, o_ref, wb_refs = refs[:n_w], refs[n_w], refs[n_w + 1:]

    @pl.when(pl.program_id(1) == 0)
    def _cast_weights():
        for w_ref, wb_ref in zip(w_refs, wb_refs):
            wb_ref[...] = w_ref[...].astype(BF16)

    a = a_ref[...]
    dims = (((1,), (1,)), ((), ())) if w_transposed else (((1,), (0,)), ((), ()))
    accs = [lax.dot_general(a, wb_ref[...], dims, preferred_element_type=F32) for wb_ref in wb_refs]
    o_ref[...] = epilogue(*accs).astype(o_ref.dtype)


def _proj(a, w, offsets, n, tn, epilogue, out_dtype, name, w_transposed=False):
    m, k = a.shape
    tm = _tile(m, 1024)
    assert n % tn == 0

    def w_spec(off):
        if not w_transposed:
            assert off % tn == 0
            return pl.BlockSpec((k, tn), lambda j, i: (0, off // tn + j))
        if off % tn == 0:
            return pl.BlockSpec((tn, k), lambda j, i: (off // tn + j, 0))
        sub = 8
        assert off % sub == 0 and tn % sub == 0
        return pl.BlockSpec((pl.Element(tn), pl.Element(k)),
                            lambda j, i: ((off // sub + j * (tn // sub)) * sub, 0))

    wb_shape = (tn, k) if w_transposed else (k, tn)
    kern = functools.partial(_proj_kernel, n_w=len(offsets), w_transposed=w_transposed, epilogue=epilogue)
    return pl.pallas_call(
        kern,
        grid=(n // tn, m // tm),
        in_specs=[pl.BlockSpec((tm, k), lambda j, i: (i, 0))] + [w_spec(off) for off in offsets],
        out_specs=pl.BlockSpec((tm, tn), lambda j, i: (i, j)),
        out_shape=jax.ShapeDtypeStruct((m, n), out_dtype),
        scratch_shapes=[pltpu.VMEM(wb_shape, BF16) for _ in offsets],
        compiler_params=_params(2),
        name=name,
    )(a, *([w] * len(offsets)))


def _identity(x):
    return x


def _og_gate(o, z):
    return jax.nn.sigmoid(o) * _silu(z)


def _uz_gate(u, z):
    return jax.nn.gelu(u) * _silu(z)


def _conv3_silu(x, w):
    s = x.shape[0]
    row = lax.broadcasted_iota(jnp.int32, x.shape, 0)
    x_prev = jnp.where(row == 0, 0.0, pltpu.roll(x, 1, axis=0))
    x_next = jnp.where(row == s - 1, 0.0, pltpu.roll(x, s - 1, axis=0))
    return _silu(x_prev * w[0:1] + x * w[1:2] + x_next * w[2:3])


def _conv_kernel(xq_ref, xk_ref, wq_ref, wk_ref, q_ref, k_ref, kt_ref, *, q_scale):
    q_ref[0] = (_conv3_silu(xq_ref[0].astype(F32), wq_ref[...]) * q_scale).astype(BF16)
    k = _conv3_silu(xk_ref[0].astype(F32), wk_ref[...])
    k_ref[0] = k.astype(BF16)
    l = kt_ref.shape[3]
    for c in range(kt_ref.shape[1]):
        kt_ref[0, c] = k[c * l:(c + 1) * l, :].T.astype(BF16)


def _conv_silu(proj, conv_w, qk_width, dk):
    b, s, _ = proj.shape
    assert conv_w.shape[0] == 3, "centred depthwise conv is written for 3 taps"
    tc = _tile(qk_width, 256)
    nq = qk_width // tc
    l = MLSTM_CHUNK
    kern = functools.partial(_conv_kernel, q_scale=float(dk) ** -0.5)
    col = pl.BlockSpec((1, s, tc), lambda i, j: (i, 0, j))
    return pl.pallas_call(
        kern,
        grid=(b, nq),
        in_specs=[col, pl.BlockSpec((1, s, tc), lambda i, j: (i, 0, nq + j)),
                  pl.BlockSpec((3, tc), lambda i, j: (0, j)), pl.BlockSpec((3, tc), lambda i, j: (0, nq + j))],
        out_specs=[col, col, pl.BlockSpec((1, s // l, tc, l), lambda i, j: (i, 0, j, 0))],
        out_shape=[jax.ShapeDtypeStruct((b, s, qk_width), BF16), jax.ShapeDtypeStruct((b, s, qk_width), BF16),
                   jax.ShapeDtypeStruct((b, s // l, qk_width, l), BF16)],
        compiler_params=_params(2),
        name="qk_conv_silu",
    )(proj, proj, conv_w, conv_w)


def _gates_kernel(x_ref, b_ref, ig_ref, pre_ref, suf_ref):
    nc, r, l = x_ref.shape[1:]
    x = x_ref[0] + b_ref[...][None]
    ig_ref[0] = x
    lf = (jnp.minimum(x, 0.0) - jnp.log1p(jnp.exp(-jnp.abs(x)))).reshape(nc * r, l)
    u = lax.broadcasted_iota(jnp.int32, (l, l), 0)
    t = lax.broadcasted_iota(jnp.int32, (l, l), 1)
    pre = jnp.dot(lf, (u <= t).astype(F32), precision=HIGHEST, preferred_element_type=F32)
    suf = jnp.dot(lf, (u >= t).astype(F32), precision=HIGHEST, preferred_element_type=F32)
    pre_ref[0] = pre.reshape(nc, r, l)
    suf_ref[0] = suf.reshape(nc, r, l)


def _gate_prep(graw, bias):
    b, nc, r, l = graw.shape
    spec = pl.BlockSpec((1, nc, r, l), lambda i: (i, 0, 0, 0))
    shape = jax.ShapeDtypeStruct(graw.shape, F32)
    return pl.pallas_call(
        _gates_kernel,
        grid=(b,),
        in_specs=[spec, pl.BlockSpec((r, 1), lambda i: (0, 0))],
        out_specs=[spec, spec, spec],
        out_shape=[shape, shape, shape],
        compiler_params=_params(1),
        name="mlstm_gate_prep",
    )(graw, bias.reshape(r, 1))


def _row_to_col(row_vec, eye):
    return jnp.sum(jnp.where(eye, row_vec, 0.0), axis=1, keepdims=True)


def _mlstm_chunk(q, k, kt, v, ig, g, tot, m, n, c_ref, *, causal):
    l = q.shape[0]
    r_i = lax.broadcasted_iota(jnp.int32, (l, l), 0)
    c_i = lax.broadcasted_iota(jnp.int32, (l, l), 1)
    eye = r_i == c_i
    g_col = _row_to_col(g, eye)
    d = g_col - g + ig
    d = jnp.where((c_i <= r_i) if causal else (c_i >= r_i), d, -jnp.inf)
    gm = g_col + m
    m_t = jnp.maximum(gm, jnp.max(d, axis=1, keepdims=True))
    p = jnp.exp(d - m_t)
    inter = jnp.exp(gm - m_t)
    s_ = jnp.dot(q, kt, preferred_element_type=F32) * p
    qc = jnp.dot(q, c_ref[...].astype(BF16), preferred_element_type=F32)
    num = jnp.dot(s_.astype(BF16), v, preferred_element_type=F32) + inter * qc
    qn = jnp.sum(q.astype(F32) * n, axis=1, keepdims=True)
    den = jnp.sum(s_, axis=1, keepdims=True) + inter * qn
    h = num * (1.0 / jnp.maximum(jnp.abs(den), jnp.exp(-m_t)))
    ds = tot - g + ig
    m_new = jnp.maximum(tot + m, jnp.max(ds, axis=1, keepdims=True))
    w = jnp.exp(ds - m_new)
    decay = jnp.exp(tot + m - m_new)
    wkt = (kt.astype(F32) * w).astype(BF16)
    c_ref[...] = decay * c_ref[...] + jnp.dot(wkt, v, preferred_element_type=F32)
    w_rows = jnp.broadcast_to(w, (8, l)).astype(BF16)
    n_new = decay * n + jnp.dot(w_rows, k, preferred_element_type=F32)[0:1]
    return h, m_new, n_new


MLSTM_UNROLL = 2


def _mlstm_kernel(q_ref, k_ref, kt_ref, v_ref, og_ref, gp_ref, ng_ref, o_ref, acc_ref, cf_ref, cb_ref):
    s, dv = v_ref.shape[1:]
    dk = q_ref.shape[2]
    l = MLSTM_CHUNK
    nc = s // l
    cf_ref[...] = jnp.zeros_like(cf_ref)
    cb_ref[...] = jnp.zeros_like(cb_ref)
    gain = ng_ref[0]

    def direction(c, state, c_ref, causal):
        m, n = state
        rows = pl.ds(pl.multiple_of(c * l, l), l)
        q, k, v = q_ref[0, rows, :], k_ref[0, rows, :], v_ref[0, rows, :]
        gp = gp_ref[0, 0, c]
        if causal:
            ig, g = gp[0:1], gp[1:2]
            tot = g[:, l - 1:l]
        else:
            ig, g = gp[2:3], gp[3:4]
            tot = g[:, 0:1]
        h, m_new, n_new = _mlstm_chunk(q, k, kt_ref[0, c], v, ig, g, tot, m, n, c_ref, causal=causal)
        return rows, h, (m_new, n_new)

    def emit(rows, h, first_touch):
        if first_touch:
            acc_ref[rows, :] = h
        else:
            y = _rms(acc_ref[rows, :] + h) * gain * og_ref[0, rows, :].astype(F32)
            o_ref[0, rows, :] = y.astype(o_ref.dtype)

    def body(first_touch, i, carry):
        st_f, st_b = carry
        for u in range(MLSTM_UNROLL):
            c = i * MLSTM_UNROLL + u
            rows_f, h_f, st_f = direction(c, st_f, cf_ref, True)
            emit(rows_f, h_f, first_touch)
            rows_b, h_b, st_b = direction(nc - 1 - c, st_b, cb_ref, False)
            emit(rows_b, h_b, first_touch)
        return st_f, st_b

    init = (jnp.zeros((1, 1), F32), jnp.zeros((1, dk), F32))
    n_it = nc // MLSTM_UNROLL
    carry = lax.fori_loop(0, n_it // 2, functools.partial(body, True), (init, init))
    lax.fori_loop(n_it // 2, n_it, functools.partial(body, False), carry)


def _mlstm(q, k, kt, qkv, og, gate_pack, norm_g, heads, dk, dv, v_col0):
    b, s, _ = qkv.shape
    assert s % (2 * MLSTM_UNROLL * MLSTM_CHUNK) == 0 and v_col0 % dv == 0
    nc = s // MLSTM_CHUNK
    seq_dk = pl.BlockSpec((1, s, dk), lambda i, h: (i, 0, h))
    seq_dv = pl.BlockSpec((1, s, dv), lambda i, h: (i, 0, h))
    return pl.pallas_call(
        _mlstm_kernel,
        grid=(b, heads),
        in_specs=[seq_dk, seq_dk,
                  pl.BlockSpec((1, nc, dk, MLSTM_CHUNK), lambda i, h: (i, 0, h, 0)),
                  pl.BlockSpec((1, s, dv), lambda i, h: (i, 0, v_col0 // dv + h)),
                  seq_dv,
                  pl.BlockSpec((1, 1, nc, 8, MLSTM_CHUNK), lambda i, h: (i, h, 0, 0, 0)),
                  pl.BlockSpec((1, 1, dv), lambda i, h: (h, 0, 0))],
        out_specs=seq_dv,
        out_shape=jax.ShapeDtypeStruct((b, s, heads * dv), BF16),
        scratch_shapes=[pltpu.VMEM((s, dv), F32),
                        pltpu.VMEM((dk, dv), F32),
                        pltpu.VMEM((dk, dv), F32)],
        compiler_params=_params(2),
        name="mlstm_bidirectional",
    )(q, k, kt, qkv, og, gate_pack, norm_g.reshape(heads, 1, dv))


def _rpb_toeplitz_kernel(r_ref, o_ref, *, win_c):
    dpad, n = r_ref.shape[1], o_ref.shape[1]
    dd = lax.broadcasted_iota(jnp.int32, (dpad, n), 0)
    cc = lax.broadcasted_iota(jnp.int32, (dpad, n), 1)
    shift = GRID_W.bit_length() - 1
    qc, kc = cc >> shift, cc & (GRID_W - 1)
    dc = jnp.clip(kc - qc + (win_c - 1), 0, 2 * win_c - 2)
    onehot = jnp.where(dd == dc, 1.0, 0.0).astype(F32)
    vals = jnp.dot(r_ref[...], onehot, precision=HIGHEST, preferred_element_type=F32)
    co = lax.broadcasted_iota(jnp.int32, (1, n), 1)
    qo, ko = co >> shift, co & (GRID_W - 1)
    cs = jnp.clip(qo - win_c // 2, 0, GRID_W - win_c)
    o_ref[...] = jnp.where((ko >= cs) & (ko < cs + win_c), vals, -jnp.inf)


def _natten_plan(n_rows, win_r, q_rows, k_rows):
    variants, var_idx, k_start = [], [], []
    for r0 in range(0, n_rows, q_rows):
        kp = min(max(r0 - win_r // 2, 0), n_rows - k_rows)
        pattern = []
        for i in range(q_rows):
            r = r0 + i
            rs = min(max(r - win_r // 2, 0), n_rows - win_r)
            assert kp <= rs and rs + win_r <= kp + k_rows
            pattern.append(tuple((kp + jj - r + win_r - 1) if rs <= kp + jj < rs + win_r else None
                                 for jj in range(k_rows)))
        pattern = tuple(pattern)
        if pattern not in variants:
            variants.append(pattern)
        var_idx.append(variants.index(pattern))
        k_start.append(kp)
    return variants, var_idx, k_start


def _bias_assemble_kernel(t_ref, o_ref, *, variants):
    w = GRID_W
    masked = jnp.full((w, w), -jnp.inf, F32)
    for v, pattern in enumerate(variants):
        for i, row in enumerate(pattern):
            tiles = [masked if dr is None else t_ref[0, dr] for dr in row]
            o_ref[0, v, i * w:(i + 1) * w, :] = jnp.concatenate(tiles, axis=1)


def _natten_bias(rpb, variants):
    hb, n_dr, n_dc = rpb.shape
    win_c = (n_dc + 1) // 2
    w = GRID_W
    dpad = 32
    assert n_dc <= dpad and (hb * n_dr) % 8 == 0
    r = jnp.pad(rpb, ((0, 0), (0, 0), (0, dpad - n_dc))).reshape(hb * n_dr, dpad)
    toeplitz = pl.pallas_call(
        functools.partial(_rpb_toeplitz_kernel, win_c=win_c),
        grid=(1,),
        in_specs=[pl.BlockSpec(r.shape, lambda i: (0, 0))],
        out_specs=pl.BlockSpec((r.shape[0], w * w), lambda i: (0, 0)),
        out_shape=jax.ShapeDtypeStruct((r.shape[0], w * w), F32),
        compiler_params=_params(1),
        name="natten_rpb_toeplitz",
    )(r).reshape(hb, n_dr, w, w)
    q_rows, k_rows = len(variants[0]), len(variants[0][0])
    out_blk = (1, len(variants), q_rows * w, k_rows * w)
    return pl.pallas_call(
        functools.partial(_bias_assemble_kernel, variants=variants),
        grid=(hb,),
        in_specs=[pl.BlockSpec((1, n_dr, w, w), lambda h: (h, 0, 0, 0))],
        out_specs=pl.BlockSpec(out_blk, lambda h: (h, 0, 0, 0)),
        out_shape=jax.ShapeDtypeStruct((hb,) + out_blk[1:], F32),
        compiler_params=_params(1),
        name="natten_bias_assemble",
    )(toeplitz)


NATTEN_Q_ROWS = 2
NATTEN_BATCH = 4


def _natten_kernel(var_ref, ks_ref, q_ref, k_ref, v_ref, z_ref, bias_ref, qg_ref, kg_ref, o_ref,
                   qn_ref, kn_ref, sc_a, sc_b):
    s, hd = q_ref.shape[1:]
    w = GRID_W
    nq, nk = bias_ref.shape[2:]
    n_batches = s // (nq * NATTEN_BATCH)
    qn_ref[...] = (_rms(q_ref[0].astype(F32)) * qg_ref[...] * (float(hd) ** -0.5)).astype(BF16)
    kn_ref[...] = (_rms(k_ref[0].astype(F32)) * kg_ref[...]).astype(BF16)

    def rows_of(p):
        return (pl.ds(pl.multiple_of(p * nq, nq), nq), pl.ds(pl.multiple_of(ks_ref[p] * w, w), nk))

    def scores(p):
        q_rows, kv_rows = rows_of(p)
        qk = lax.dot_general(qn_ref[q_rows, :], kn_ref[kv_rows, :], (((1,), (1,)), ((), ())),
                             preferred_element_type=F32)
        return qk + bias_ref[0, var_ref[p]]

    def attend(p, sc):
        q_rows, kv_rows = rows_of(p)
        p_un = jnp.exp(sc - jnp.max(sc, axis=1, keepdims=True))
        o = jnp.dot(p_un.astype(BF16), v_ref[0, kv_rows, :], preferred_element_type=F32)
        o = o * (1.0 / jnp.sum(p_un, axis=1, keepdims=True))
        o_ref[0, q_rows, :] = (o * _silu(z_ref[0, q_rows, :].astype(F32))).astype(o_ref.dtype)

    def fill(buf, j):
        for u in range(NATTEN_BATCH):
            buf[u] = scores(j * NATTEN_BATCH + u)

    def drain(buf, j):
        for u in range(NATTEN_BATCH):
            attend(j * NATTEN_BATCH + u, buf[u])

    def body(k, carry):
        fill(sc_b, 2 * k + 1)
        drain(sc_a, 2 * k)
        fill(sc_a, 2 * k + 2)
        drain(sc_b, 2 * k + 1)
        return carry

    fill(sc_a, 0)
    lax.fori_loop(0, n_batches // 2 - 1, body, 0)
    fill(sc_b, n_batches - 1)
    drain(sc_a, n_batches - 2)
    drain(sc_b, n_batches - 1)


def _natten(proj, rpb, q_gain, k_gain, heads, hd, col0):
    b, s, _ = proj.shape
    win_r = (rpb.shape[1] + 1) // 2
    n_rows = s // GRID_W
    k_rows = win_r + NATTEN_Q_ROWS
    assert s % GRID_W == 0 and n_rows >= k_rows and col0 % hd == 0
    assert n_rows % (2 * NATTEN_Q_ROWS * NATTEN_BATCH) == 0
    variants, var_idx, k_start = _natten_plan(n_rows, win_r, NATTEN_Q_ROWS, k_rows)
    bias = _natten_bias(rpb, variants)
    sc_buf = pltpu.VMEM((NATTEN_BATCH,) + bias.shape[2:], F32)
    c0 = col0 // hd

    def col(part):
        return pl.BlockSpec((1, s, hd), lambda h, i: (i, 0, c0 + part * heads + h))

    smem = pl.BlockSpec(memory_space=pltpu.SMEM)
    vec = pl.BlockSpec((1, hd), lambda h, i: (0, 0))
    return pl.pallas_call(
        _natten_kernel,
        grid=(heads, b),
        in_specs=[smem, smem, col(0), col(1), col(2), col(3),
                  pl.BlockSpec((1,) + bias.shape[1:], lambda h, i: (h, 0, 0, 0)), vec, vec],
        out_specs=pl.BlockSpec((1, s, hd), lambda h, i: (i, 0, h)),
        out_shape=jax.ShapeDtypeStruct((b, s, heads * hd), BF16),
        scratch_shapes=[pltpu.VMEM((s, hd), BF16), pltpu.VMEM((s, hd), BF16), sc_buf, sc_buf],
        compiler_params=_params(2),
        name="neighbourhood_attention",
    )(jnp.asarray(var_idx, jnp.int32), jnp.asarray(k_start, jnp.int32), proj, proj, proj, proj, bias,
      q_gain.reshape(1, hd), k_gain.reshape(1, hd))


def _mm_out_kernel(a1_ref, a2_ref, w1_ref, w2_ref, x_ref, gate_ref, o_ref):
    acc = jnp.dot(a1_ref[0], w1_ref[...], preferred_element_type=F32)
    acc = acc + jnp.dot(a2_ref[0], w2_ref[...], preferred_element_type=F32)
    o_ref[0] = x_ref[0] + gate_ref[0] * acc


def _out_proj(a1, a1_blk, a2, a2_blk, w, x, gate, name):
    b, s, d = x.shape
    kh = w.shape[0] // 2
    tm, tn = _tile(s, 512), _tile(d, 512)
    return pl.pallas_call(
        _mm_out_kernel,
        grid=(b, s // tm, d // tn),
        in_specs=[pl.BlockSpec((1, tm, kh), lambda i, m, n: (i, m, a1_blk)),
                  pl.BlockSpec((1, tm, kh), lambda i, m, n: (i, m, a2_blk)),
                  pl.BlockSpec((kh, tn), lambda i, m, n: (0, n)),
                  pl.BlockSpec((kh, tn), lambda i, m, n: (1, n)),
                  pl.BlockSpec((1, tm, tn), lambda i, m, n: (i, m, n)),
                  pl.BlockSpec((1, 1, tn), lambda i, m, n: (i, 0, n))],
        out_specs=pl.BlockSpec((1, tm, tn), lambda i, m, n: (i, m, n)),
        out_shape=jax.ShapeDtypeStruct((b, s, d), F32),
        compiler_params=_params(3),
        name=name,
    )(a1, a2, w, w, x, gate.reshape(b, 1, d))


def _spatial_kernel(gv_ref, uz_ref, ws_ref, bs_ref, vg_ref, o_ref):
    groups = ws_ref.shape[0]
    cg = gv_ref.shape[2] // groups
    vn = (_rms(gv_ref[0].astype(F32)) * vg_ref[...]).astype(BF16)
    for g in range(groups):
        cols = slice(g * cg, (g + 1) * cg)
        sv = jnp.dot(ws_ref[g], vn[:, cols], preferred_element_type=F32) + bs_ref[:, g:g + 1]
        o_ref[0, :, cols] = (uz_ref[0, :, cols].astype(F32) * sv).astype(BF16)


def _spatial(gv, uz, ws, bs_t, v_gain):
    b, s, cw = gv.shape
    l = SPATIAL_CHUNK
    groups = ws.shape[0]
    assert s % l == 0 and ws.shape[1:] == (l, l)
    blk = pl.BlockSpec((1, l, cw), lambda i, n: (i, n, 0))
    return pl.pallas_call(
        _spatial_kernel,
        grid=(b, s // l),
        in_specs=[blk, blk,
                  pl.BlockSpec(ws.shape, lambda i, n: (0, 0, 0)),
                  pl.BlockSpec((l, groups), lambda i, n: (0, 0)),
                  pl.BlockSpec((1, cw), lambda i, n: (0, 0))],
        out_specs=blk,
        out_shape=jax.ShapeDtypeStruct((b, s, cw), BF16),
        compiler_params=_params(2),
        name="gmlp_spatial_gate",
    )(gv, uz, ws, bs_t, v_gain.reshape(1, cw))


def kernel(x, c, norm_g0, ada_w0, ada_b0, w_in0, a_conv_w, a_gate_b, a_norm_g, b_q_gain, b_k_gain, b_rpb, w_out0,
           norm_g1, ada_w1, ada_b1, w_in1, c_v_norm_g, c_w_s, c_b_s, w_out1):
    b, s, d = x.shape
    a_heads, dv = a_norm_g.shape
    a_width = a_heads * dv
    dk = dv // 2
    qk_width = a_heads * dk
    hd = b_q_gain.shape[0]
    b_heads, b_width = b_rpb.shape[0], b_rpb.shape[0] * hd
    n_gate = 4 * a_heads
    a_cols = 2 * qk_width + 3 * a_width
    assert w_in0.shape[1] == a_cols + n_gate + 4 * b_width
    cw = c_v_norm_g.shape[0]
    nc = s // MLSTM_CHUNK

    c_pad = jnp.pad(c, ((0, -b % 8), (0, 0)))
    mod0 = _modulation(c_pad, ada_w0, ada_b0)[:b]
    mod1 = _modulation(c_pad, ada_w1, ada_b1)[:b]
    shift0, scale0, gate0 = mod0[:, :d], mod0[:, d:2 * d], mod0[:, 2 * d:]
    shift1, scale1, gate1 = mod1[:, :d], mod1[:, d:2 * d], mod1[:, 2 * d:]

    h0 = _norm_mod(x, norm_g0, scale0, shift0).reshape(b * s, d)
    qkv_cols = 2 * qk_width + a_width
    wt0 = w_in0.T
    qkv = _proj(h0, wt0, [0], qkv_cols, _tile(qkv_cols, 512), _identity, BF16, "layer0_in_proj_qkv",
                w_transposed=True).reshape(b, s, qkv_cols)
    og = _proj(h0, wt0, [qkv_cols, qkv_cols + a_width], a_width, _tile(a_width, 256), _og_gate, BF16,
               "layer0_in_proj_og", w_transposed=True).reshape(b, s, a_width)
    proj_b = _proj(h0, wt0, [a_cols + n_gate], 4 * b_width, _tile(4 * b_width, 512), _identity, BF16,
                   "layer0_in_proj_b", w_transposed=True).reshape(b, s, 4 * b_width)
    assert a_cols % LANES == 0 and n_gate <= LANES
    gates = _proj(h0, wt0, [a_cols], LANES, LANES, _identity, F32, "layer0_gate_proj",
                  w_transposed=True)[:, :n_gate]

    q, k, kt = _conv_silu(qkv, a_conv_w, qk_width, dk)
    graw = gates.reshape(b, nc, MLSTM_CHUNK, n_gate).transpose(0, 1, 3, 2)
    ig, pre, suf = _gate_prep(graw, a_gate_b)
    hh = a_heads
    zeros = jnp.zeros_like(ig[:, :, :hh])
    pack = jnp.stack([ig[:, :, :hh], pre[:, :, hh:2 * hh], ig[:, :, 2 * hh:3 * hh], suf[:, :, 3 * hh:],
                      zeros, zeros, zeros, zeros], axis=3)
    pack = pack.transpose(0, 2, 1, 3, 4)
    y_a = _mlstm(q, k, kt, qkv, og, pack, a_norm_g, a_heads, dk, dv, 2 * qk_width)

    y_b = _natten(proj_b, b_rpb, b_q_gain, b_k_gain, b_heads, hd, 0)

    assert a_width == b_width
    x1 = _out_proj(y_a, 0, y_b, 0, w_out0.astype(BF16), x, gate0, "layer0_out_proj")

    h1 = _norm_mod(x1, norm_g1, scale1, shift1).reshape(b * s, d)
    uz = _proj(h1, w_in1, [0, 2 * cw], cw, _tile(cw, 256), _uz_gate, BF16, "gmlp_in_proj_uz")
    gv = _proj(h1, w_in1, [cw], cw, _tile(cw, 512), jax.nn.gelu, BF16, "gmlp_in_proj_v")
    y = _spatial(gv.reshape(b, s, cw), uz.reshape(b, s, cw), c_w_s.astype(BF16), c_b_s.T, c_v_norm_g)
    return _out_proj(y, 0, y, 1, w_out1.astype(BF16), x1, gate1, "layer1_out_proj")
```

```python
import functools

import jax
import jax.numpy as jnp
from jax import lax
from jax.experimental import pallas as pl
from jax.experimental.pallas import tpu as pltpu

F32 = jnp.float32
BF16 = jnp.bfloat16
HIGHEST = lax.Precision.HIGHEST

EPS = 1e-6
GRID_W = 64
MLSTM_CHUNK = 128
SPATIAL_CHUNK = 128
LANES = 128
V7X_VMEM_LIMIT_BYTES = 52 * 1024 * 1024


def _params(n_axes, vmem=V7X_VMEM_LIMIT_BYTES):
    return pltpu.CompilerParams(dimension_semantics=("arbitrary",) * n_axes, vmem_limit_bytes=vmem)


def _tile(dim, pref):
    t = min(dim, pref)
    assert dim % t == 0, (dim, pref)
    return t


def _silu(x):
    return x * jax.nn.sigmoid(x)


def _rms(x):
    return x * lax.rsqrt(jnp.mean(x * x, axis=-1, keepdims=True) + EPS)


def _ada_kernel(c_ref, w_ref, b_ref, o_ref):
    a = _silu(c_ref[...]).astype(BF16)
    o_ref[...] = jnp.dot(a, w_ref[...].astype(BF16), preferred_element_type=F32) + b_ref[...]


def _modulation(c_pad, w, b):
    rows, d = c_pad.shape
    n = w.shape[1]
    tn = _tile(n, 512)
    return pl.pallas_call(
        _ada_kernel,
        grid=(n // tn,),
        in_specs=[pl.BlockSpec((rows, d), lambda j: (0, 0)),
                  pl.BlockSpec((d, tn), lambda j: (0, j)),
                  pl.BlockSpec((1, tn), lambda j: (0, j))],
        out_specs=pl.BlockSpec((rows, tn), lambda j: (0, j)),
        out_shape=jax.ShapeDtypeStruct((rows, n), F32),
        compiler_params=_params(1),
        name="ada_modulation",
    )(c_pad, w, b.reshape(1, n))


def _norm_mod_kernel(x_ref, g_ref, sc_ref, sh_ref, o_ref):
    y = _rms(x_ref[0]) * g_ref[...]
    o_ref[0] = (y * (1.0 + sc_ref[0]) + sh_ref[0]).astype(BF16)


def _norm_mod(x, g, scale, shift):
    b, s, d = x.shape
    ts = _tile(s, 256)
    vec = pl.BlockSpec((1, 1, d), lambda i, j: (i, 0, 0))
    return pl.pallas_call(
        _norm_mod_kernel,
        grid=(b, s // ts),
        in_specs=[pl.BlockSpec((1, ts, d), lambda i, j: (i, j, 0)),
                  pl.BlockSpec((1, d), lambda i, j: (0, 0)), vec, vec],
        out_specs=pl.BlockSpec((1, ts, d), lambda i, j: (i, j, 0)),
        out_shape=jax.ShapeDtypeStruct((b, s, d), BF16),
        compiler_params=_params(2),
        name="norm_modulate",
    )(x, g.reshape(1, d), scale.reshape(b, 1, d), shift.reshape(b, 1, d))


BF16_SUBLANES = 16


def _proj_kernel(a_ref, *refs, w_transposed, epilogue):
    w_refs, o_ref = refs[:-1], refs[-1]
    a = a_ref[...]
    dims = (((1,), (1,)), ((), ())) if w_transposed else (((1,), (0,)), ((), ()))
    accs = [lax.dot_general(a, w_ref[...], dims, preferred_element_type=F32) for w_ref in w_refs]
    o_ref[...] = epilogue(*accs).astype(o_ref.dtype)


def _proj(a, w, offsets, n, tn, epilogue, out_dtype, name, w_transposed=False):
    m, k = a.shape
    tm = _tile(m, 1024)
    assert n % tn == 0

    def w_spec(off):
        if not w_transposed:
            assert off % tn == 0
            return pl.BlockSpec((k, tn), lambda i, j: (0, off // tn + j))
        if off % tn == 0:
            return pl.BlockSpec((tn, k), lambda i, j: (off // tn + j, 0))
        sub = BF16_SUBLANES
        assert off % sub == 0 and tn % sub == 0
        return pl.BlockSpec((pl.Element(tn), pl.Element(k)),
                            lambda i, j: ((off // sub + j * (tn // sub)) * sub, 0))

    kern = functools.partial(_proj_kernel, w_transposed=w_transposed, epilogue=epilogue)
    return pl.pallas_call(
        kern,
        grid=(m // tm, n // tn),
        in_specs=[pl.BlockSpec((tm, k), lambda i, j: (i, 0))] + [w_spec(off) for off in offsets],
        out_specs=pl.BlockSpec((tm, tn), lambda i, j: (i, j)),
        out_shape=jax.ShapeDtypeStruct((m, n), out_dtype),
        compiler_params=_params(2),
        name=name,
    )(a, *([w] * len(offsets)))


def _identity(x):
    return x


def _og_gate(o, z):
    return jax.nn.sigmoid(o) * _silu(z)


def _uz_gate(u, z):
    return jax.nn.gelu(u) * _silu(z)


def _conv3_silu(x, w):
    s = x.shape[0]
    row = lax.broadcasted_iota(jnp.int32, x.shape, 0)
    x_prev = jnp.where(row == 0, 0.0, pltpu.roll(x, 1, axis=0))
    x_next = jnp.where(row == s - 1, 0.0, pltpu.roll(x, s - 1, axis=0))
    return _silu(x_prev * w[0:1] + x * w[1:2] + x_next * w[2:3])


def _conv_kernel(xq_ref, xk_ref, wq_ref, wk_ref, q_ref, k_ref, kt_ref, *, q_scale):
    q_ref[0] = (_conv3_silu(xq_ref[0].astype(F32), wq_ref[...]) * q_scale).astype(BF16)
    k = _conv3_silu(xk_ref[0].astype(F32), wk_ref[...])
    k_ref[0] = k.astype(BF16)
    l = kt_ref.shape[3]
    for c in range(kt_ref.shape[1]):
        kt_ref[0, c] = k[c * l:(c + 1) * l, :].T.astype(BF16)


def _conv_silu(proj, conv_w, qk_width, dk):
    b, s, _ = proj.shape
    assert conv_w.shape[0] == 3, "centred depthwise conv is written for 3 taps"
    tc = _tile(qk_width, 256)
    nq = qk_width // tc
    l = MLSTM_CHUNK
    kern = functools.partial(_conv_kernel, q_scale=float(dk) ** -0.5)
    col = pl.BlockSpec((1, s, tc), lambda i, j: (i, 0, j))
    return pl.pallas_call(
        kern,
        grid=(b, nq),
        in_specs=[col, pl.BlockSpec((1, s, tc), lambda i, j: (i, 0, nq + j)),
                  pl.BlockSpec((3, tc), lambda i, j: (0, j)), pl.BlockSpec((3, tc), lambda i, j: (0, nq + j))],
        out_specs=[col, col, pl.BlockSpec((1, s // l, tc, l), lambda i, j: (i, 0, j, 0))],
        out_shape=[jax.ShapeDtypeStruct((b, s, qk_width), BF16), jax.ShapeDtypeStruct((b, s, qk_width), BF16),
                   jax.ShapeDtypeStruct((b, s // l, qk_width, l), BF16)],
        compiler_params=_params(2),
        name="qk_conv_silu",
    )(proj, proj, conv_w, conv_w)


def _gates_kernel(x_ref, b_ref, ig_ref, pre_ref, suf_ref, m_ref):
    nc, r, l = x_ref.shape[1:]
    hh = r // 4
    x = x_ref[0] + b_ref[...][None]
    ig_ref[0] = x
    lf = (jnp.minimum(x, 0.0) - jnp.log1p(jnp.exp(-jnp.abs(x)))).reshape(nc * r, l)
    u = lax.broadcasted_iota(jnp.int32, (l, l), 0)
    t = lax.broadcasted_iota(jnp.int32, (l, l), 1)
    pre = jnp.dot(lf, (u <= t).astype(F32), precision=HIGHEST, preferred_element_type=F32).reshape(nc, r, l)
    suf = jnp.dot(lf, (u >= t).astype(F32), precision=HIGHEST, preferred_element_type=F32).reshape(nc, r, l)
    pre_ref[0] = pre
    suf_ref[0] = suf
    g_f, g_b = pre[:, hh:2 * hh], suf[:, 3 * hh:]
    tot_f, tot_b = g_f[:, :, l - 1:l], g_b[:, :, 0:1]
    a_f = jnp.max(tot_f - g_f + x[:, :hh], axis=2, keepdims=True)
    a_b = jnp.max(tot_b - g_b + x[:, 2 * hh:3 * hh], axis=2, keepdims=True)
    m = jnp.zeros((hh, 1), F32)
    for c in range(nc):
        m_ref[0, c, 0:hh] = jnp.broadcast_to(m, (hh, l))
        m = jnp.maximum(tot_f[c] + m, a_f[c])
        m_ref[0, c, hh:2 * hh] = jnp.broadcast_to(m, (hh, l))
    m = jnp.zeros((hh, 1), F32)
    for c in reversed(range(nc)):
        m_ref[0, c, 2 * hh:3 * hh] = jnp.broadcast_to(m, (hh, l))
        m = jnp.maximum(tot_b[c] + m, a_b[c])
        m_ref[0, c, 3 * hh:] = jnp.broadcast_to(m, (hh, l))


def _gate_prep(graw, bias):
    b, nc, r, l = graw.shape
    spec = pl.BlockSpec((1, nc, r, l), lambda i: (i, 0, 0, 0))
    shape = jax.ShapeDtypeStruct(graw.shape, F32)
    return pl.pallas_call(
        _gates_kernel,
        grid=(b,),
        in_specs=[spec, pl.BlockSpec((r, 1), lambda i: (0, 0))],
        out_specs=[spec] * 4,
        out_shape=[shape] * 4,
        compiler_params=_params(1),
        name="mlstm_gate_prep",
    )(graw, bias.reshape(r, 1))


def _row_to_col(row_vec, eye):
    return jnp.sum(jnp.where(eye, row_vec, 0.0), axis=1, keepdims=True)


def _mlstm_local(q, kt, ig, g, tot, m, m_new, *, causal):
    l = q.shape[0]
    r_i = lax.broadcasted_iota(jnp.int32, (l, l), 0)
    c_i = lax.broadcasted_iota(jnp.int32, (l, l), 1)
    g_col = _row_to_col(g, r_i == c_i)
    d = g_col - g + ig
    d = jnp.where((c_i <= r_i) if causal else (c_i >= r_i), d, -jnp.inf)
    gm = g_col + m
    m_t = jnp.maximum(gm, jnp.max(d, axis=1, keepdims=True))
    s_ = jnp.dot(q, kt, preferred_element_type=F32) * jnp.exp(d - m_t)
    w = jnp.exp(tot - g + ig - m_new)
    return dict(
        s=s_.astype(BF16), s_sum=jnp.sum(s_, axis=1, keepdims=True),
        inter=jnp.exp(gm - m_t), floor=jnp.exp(-m_t), decay=jnp.exp(tot + m - m_new),
        wkt=(kt.astype(F32) * w).astype(BF16),
        w_rows=jnp.broadcast_to(w, (8, l)).astype(BF16))


def _mlstm_state(q, k, v, loc, c, n):
    qc = jnp.dot(q, c.astype(BF16), preferred_element_type=F32)
    num = jnp.dot(loc["s"], v, preferred_element_type=F32) + loc["inter"] * qc
    qn = jnp.sum(q.astype(F32) * n, axis=1, keepdims=True)
    den = loc["s_sum"] + loc["inter"] * qn
    h = num * (1.0 / jnp.maximum(jnp.abs(den), loc["floor"]))
    c_new = loc["decay"] * c + jnp.dot(loc["wkt"], v, preferred_element_type=F32)
    n_new = loc["decay"] * n + jnp.dot(loc["w_rows"], k, preferred_element_type=F32)[0:1]
    return h, c_new, n_new


MLSTM_UNROLL = 4


def _mlstm_kernel(q_ref, k_ref, kt_ref, v_ref, og_ref, gp_ref, ng_ref, o_ref, acc_ref, cf_ref, cb_ref):
    s, dv = v_ref.shape[1:]
    dk = q_ref.shape[2]
    l = MLSTM_CHUNK
    nc = s // l
    cf_ref[...] = jnp.zeros_like(cf_ref)
    cb_ref[...] = jnp.zeros_like(cb_ref)
    gain = ng_ref[0]

    def rows_of(c):
        return pl.ds(pl.multiple_of(c * l, l), l)

    def local(c, causal):
        gp = gp_ref[0, 0, c]
        if causal:
            ig, g, m, m_new = gp[0:1], gp[1:2], gp[4:5, 0:1], gp[5:6, 0:1]
            tot = g[:, l - 1:l]
        else:
            ig, g, m, m_new = gp[2:3], gp[3:4], gp[6:7, 0:1], gp[7:8, 0:1]
            tot = g[:, 0:1]
        return _mlstm_local(q_ref[0, rows_of(c), :], kt_ref[0, c], ig, g, tot, m, m_new, causal=causal)

    def emit(rows, h, first_touch):
        if first_touch:
            acc_ref[rows, :] = h
        else:
            y = _rms(acc_ref[rows, :] + h) * gain * og_ref[0, rows, :].astype(F32)
            o_ref[0, rows, :] = y.astype(o_ref.dtype)

    def body(first_touch, i, carry):
        chunks = [(i * MLSTM_UNROLL + u, nc - 1 - (i * MLSTM_UNROLL + u)) for u in range(MLSTM_UNROLL)]
        n_f, n_b = carry
        c_f, c_b = cf_ref[...], cb_ref[...]
        for cf, cb in chunks:
            rows = rows_of(cf)
            h, c_f, n_f = _mlstm_state(q_ref[0, rows, :], k_ref[0, rows, :], v_ref[0, rows, :],
                                       local(cf, True), c_f, n_f)
            emit(rows, h, first_touch)
            rows = rows_of(cb)
            h, c_b, n_b = _mlstm_state(q_ref[0, rows, :], k_ref[0, rows, :], v_ref[0, rows, :],
                                       local(cb, False), c_b, n_b)
            emit(rows, h, first_touch)
        cf_ref[...] = c_f
        cb_ref[...] = c_b
        return n_f, n_b

    n0 = jnp.zeros((1, dk), F32)
    n_it = nc // MLSTM_UNROLL
    carry = lax.fori_loop(0, n_it // 2, functools.partial(body, True), (n0, n0))
    lax.fori_loop(n_it // 2, n_it, functools.partial(body, False), carry)


def _mlstm(q, k, kt, qkv, og, gate_pack, norm_g, heads, dk, dv, v_col0):
    b, s, _ = qkv.shape
    assert s % (2 * MLSTM_UNROLL * MLSTM_CHUNK) == 0 and v_col0 % dv == 0
    nc = s // MLSTM_CHUNK
    seq_dk = pl.BlockSpec((1, s, dk), lambda i, h: (i, 0, h))
    seq_dv = pl.BlockSpec((1, s, dv), lambda i, h: (i, 0, h))
    return pl.pallas_call(
        _mlstm_kernel,
        grid=(b, heads),
        in_specs=[seq_dk, seq_dk,
                  pl.BlockSpec((1, nc, dk, MLSTM_CHUNK), lambda i, h: (i, 0, h, 0)),
                  pl.BlockSpec((1, s, dv), lambda i, h: (i, 0, v_col0 // dv + h)),
                  seq_dv,
                  pl.BlockSpec((1, 1, nc, 8, MLSTM_CHUNK), lambda i, h: (i, h, 0, 0, 0)),
                  pl.BlockSpec((1, 1, dv), lambda i, h: (h, 0, 0))],
        out_specs=seq_dv,
        out_shape=jax.ShapeDtypeStruct((b, s, heads * dv), BF16),
        scratch_shapes=[pltpu.VMEM((s, dv), F32),
                        pltpu.VMEM((dk, dv), F32),
                        pltpu.VMEM((dk, dv), F32)],
        compiler_params=_params(2),
        name="mlstm_bidirectional",
    )(q, k, kt, qkv, og, gate_pack, norm_g.reshape(heads, 1, dv))


def _rpb_toeplitz_kernel(r_ref, o_ref, *, win_c):
    dpad, n = r_ref.shape[1], o_ref.shape[1]
    dd = lax.broadcasted_iota(jnp.int32, (dpad, n), 0)
    cc = lax.broadcasted_iota(jnp.int32, (dpad, n), 1)
    shift = GRID_W.bit_length() - 1
    qc, kc = cc >> shift, cc & (GRID_W - 1)
    dc = jnp.clip(kc - qc + (win_c - 1), 0, 2 * win_c - 2)
    onehot = jnp.where(dd == dc, 1.0, 0.0).astype(F32)
    vals = jnp.dot(r_ref[...], onehot, precision=HIGHEST, preferred_element_type=F32)
    co = lax.broadcasted_iota(jnp.int32, (1, n), 1)
    qo, ko = co >> shift, co & (GRID_W - 1)
    cs = jnp.clip(qo - win_c // 2, 0, GRID_W - win_c)
    o_ref[...] = jnp.where((ko >= cs) & (ko < cs + win_c), vals, -jnp.inf)


def _natten_plan(n_rows, win_r, q_rows, k_rows):
    variants, var_idx, k_start = [], [], []
    for r0 in range(0, n_rows, q_rows):
        kp = min(max(r0 - win_r // 2, 0), n_rows - k_rows)
        pattern = []
        for i in range(q_rows):
            r = r0 + i
            rs = min(max(r - win_r // 2, 0), n_rows - win_r)
            assert kp <= rs and rs + win_r <= kp + k_rows
            pattern.append(tuple((kp + jj - r + win_r - 1) if rs <= kp + jj < rs + win_r else None
                                 for jj in range(k_rows)))
        pattern = tuple(pattern)
        if pattern not in variants:
            variants.append(pattern)
        var_idx.append(variants.index(pattern))
        k_start.append(kp)
    return variants, var_idx, k_start


def _bias_assemble_kernel(t_ref, o_ref, *, variants):
    w = GRID_W
    masked = jnp.full((w, w), -jnp.inf, F32)
    for v, pattern in enumerate(variants):
        for i, row in enumerate(pattern):
            tiles = [masked if dr is None else t_ref[0, dr] for dr in row]
            o_ref[0, v, i * w:(i + 1) * w, :] = jnp.concatenate(tiles, axis=1)


def _natten_bias(rpb, variants):
    hb, n_dr, n_dc = rpb.shape
    win_c = (n_dc + 1) // 2
    w = GRID_W
    dpad = 32
    assert n_dc <= dpad and (hb * n_dr) % 8 == 0
    r = jnp.pad(rpb, ((0, 0), (0, 0), (0, dpad - n_dc))).reshape(hb * n_dr, dpad)
    toeplitz = pl.pallas_call(
        functools.partial(_rpb_toeplitz_kernel, win_c=win_c),
        grid=(1,),
        in_specs=[pl.BlockSpec(r.shape, lambda i: (0, 0))],
        out_specs=pl.BlockSpec((r.shape[0], w * w), lambda i: (0, 0)),
        out_shape=jax.ShapeDtypeStruct((r.shape[0], w * w), F32),
        compiler_params=_params(1),
        name="natten_rpb_toeplitz",
    )(r).reshape(hb, n_dr, w, w)
    q_rows, k_rows = len(variants[0]), len(variants[0][0])
    out_blk = (1, len(variants), q_rows * w, k_rows * w)
    return pl.pallas_call(
        functools.partial(_bias_assemble_kernel, variants=variants),
        grid=(hb,),
        in_specs=[pl.BlockSpec((1, n_dr, w, w), lambda h: (h, 0, 0, 0))],
        out_specs=pl.BlockSpec(out_blk, lambda h: (h, 0, 0, 0)),
        out_shape=jax.ShapeDtypeStruct((hb,) + out_blk[1:], F32),
        compiler_params=_params(1),
        name="natten_bias_assemble",
    )(toeplitz)


NATTEN_Q_ROWS = 2
NATTEN_BATCH = 4


def _natten_kernel(var_ref, ks_ref, q_ref, k_ref, v_ref, z_ref, bias_ref, qg_ref, kg_ref, o_ref,
                   qn_ref, kn_ref, sc_a, sc_b):
    s, hd = q_ref.shape[1:]
    w = GRID_W
    nq, nk = bias_ref.shape[2:]
    n_batches = s // (nq * NATTEN_BATCH)
    qn_ref[...] = (_rms(q_ref[0].astype(F32)) * qg_ref[...] * (float(hd) ** -0.5)).astype(BF16)
    kn_ref[...] = (_rms(k_ref[0].astype(F32)) * kg_ref[...]).astype(BF16)

    def rows_of(p):
        return (pl.ds(pl.multiple_of(p * nq, nq), nq), pl.ds(pl.multiple_of(ks_ref[p] * w, w), nk))

    def scores(p):
        q_rows, kv_rows = rows_of(p)
        qk = lax.dot_general(qn_ref[q_rows, :], kn_ref[kv_rows, :], (((1,), (1,)), ((), ())),
                             preferred_element_type=F32)
        return qk + bias_ref[0, var_ref[p]]

    def attend(p, sc):
        q_rows, kv_rows = rows_of(p)
        p_un = jnp.exp(sc - jnp.max(sc, axis=1, keepdims=True))
        o = jnp.dot(p_un.astype(BF16), v_ref[0, kv_rows, :], preferred_element_type=F32)
        o = o * (1.0 / jnp.sum(p_un, axis=1, keepdims=True))
        o_ref[0, q_rows, :] = (o * _silu(z_ref[0, q_rows, :].astype(F32))).astype(o_ref.dtype)

    def fill(buf, j):
        for u in range(NATTEN_BATCH):
            buf[u] = scores(j * NATTEN_BATCH + u)

    def drain(buf, j):
        for u in range(NATTEN_BATCH):
            attend(j * NATTEN_BATCH + u, buf[u])

    def body(k, carry):
        fill(sc_b, 2 * k + 1)
        drain(sc_a, 2 * k)
        fill(sc_a, 2 * k + 2)
        drain(sc_b, 2 * k + 1)
        return carry

    fill(sc_a, 0)
    lax.fori_loop(0, n_batches // 2 - 1, body, 0)
    fill(sc_b, n_batches - 1)
    drain(sc_a, n_batches - 2)
    drain(sc_b, n_batches - 1)


def _natten(proj, rpb, q_gain, k_gain, heads, hd, col0):
    b, s, _ = proj.shape
    win_r = (rpb.shape[1] + 1) // 2
    n_rows = s // GRID_W
    k_rows = win_r + NATTEN_Q_ROWS
    assert s % GRID_W == 0 and n_rows >= k_rows and col0 % hd == 0
    assert n_rows % (2 * NATTEN_Q_ROWS * NATTEN_BATCH) == 0
    variants, var_idx, k_start = _natten_plan(n_rows, win_r, NATTEN_Q_ROWS, k_rows)
    bias = _natten_bias(rpb, variants)
    sc_buf = pltpu.VMEM((NATTEN_BATCH,) + bias.shape[2:], F32)
    c0 = col0 // hd

    def col(part):
        return pl.BlockSpec((1, s, hd), lambda h, i: (i, 0, c0 + part * heads + h))

    smem = pl.BlockSpec(memory_space=pltpu.SMEM)
    vec = pl.BlockSpec((1, hd), lambda h, i: (0, 0))
    return pl.pallas_call(
        _natten_kernel,
        grid=(heads, b),
        in_specs=[smem, smem, col(0), col(1), col(2), col(3),
                  pl.BlockSpec((1,) + bias.shape[1:], lambda h, i: (h, 0, 0, 0)), vec, vec],
        out_specs=pl.BlockSpec((1, s, hd), lambda h, i: (i, 0, h)),
        out_shape=jax.ShapeDtypeStruct((b, s, heads * hd), BF16),
        scratch_shapes=[pltpu.VMEM((s, hd), BF16), pltpu.VMEM((s, hd), BF16), sc_buf, sc_buf],
        compiler_params=_params(2),
        name="neighbourhood_attention",
    )(jnp.asarray(var_idx, jnp.int32), jnp.asarray(k_start, jnp.int32), proj, proj, proj, proj, bias,
      q_gain.reshape(1, hd), k_gain.reshape(1, hd))


def _mm_out_kernel(a1_ref, a2_ref, w1_ref, w2_ref, x_ref, gate_ref, o_ref):
    acc = jnp.dot(a1_ref[0], w1_ref[...], preferred_element_type=F32)
    acc = acc + jnp.dot(a2_ref[0], w2_ref[...], preferred_element_type=F32)
    o_ref[0] = x_ref[0] + gate_ref[0] * acc


def _out_proj(a1, a1_blk, a2, a2_blk, w, x, gate, name):
    b, s, d = x.shape
    kh = w.shape[0] // 2
    tm, tn = _tile(s, 512), _tile(d, 512)
    return pl.pallas_call(
        _mm_out_kernel,
        grid=(b, s // tm, d // tn),
        in_specs=[pl.BlockSpec((1, tm, kh), lambda i, m, n: (i, m, a1_blk)),
                  pl.BlockSpec((1, tm, kh), lambda i, m, n: (i, m, a2_blk)),
                  pl.BlockSpec((kh, tn), lambda i, m, n: (0, n)),
                  pl.BlockSpec((kh, tn), lambda i, m, n: (1, n)),
                  pl.BlockSpec((1, tm, tn), lambda i, m, n: (i, m, n)),
                  pl.BlockSpec((1, 1, tn), lambda i, m, n: (i, 0, n))],
        out_specs=pl.BlockSpec((1, tm, tn), lambda i, m, n: (i, m, n)),
        out_shape=jax.ShapeDtypeStruct((b, s, d), F32),
        compiler_params=_params(3),
        name=name,
    )(a1, a2, w, w, x, gate.reshape(b, 1, d))


def _spatial_kernel(gv_ref, uz_ref, ws_ref, bs_ref, vg_ref, o_ref):
    groups = ws_ref.shape[0]
    cg = gv_ref.shape[2] // groups
    vn = (_rms(gv_ref[0].astype(F32)) * vg_ref[...]).astype(BF16)
    for g in range(groups):
        cols = slice(g * cg, (g + 1) * cg)
        sv = jnp.dot(ws_ref[g], vn[:, cols], preferred_element_type=F32) + bs_ref[:, g:g + 1]
        o_ref[0, :, cols] = (uz_ref[0, :, cols].astype(F32) * sv).astype(BF16)


def _spatial(gv, uz, ws, bs_t, v_gain):
    b, s, cw = gv.shape
    l = SPATIAL_CHUNK
    groups = ws.shape[0]
    assert s % l == 0 and ws.shape[1:] == (l, l)
    blk = pl.BlockSpec((1, l, cw), lambda i, n: (i, n, 0))
    return pl.pallas_call(
        _spatial_kernel,
        grid=(b, s // l),
        in_specs=[blk, blk,
                  pl.BlockSpec(ws.shape, lambda i, n: (0, 0, 0)),
                  pl.BlockSpec((l, groups), lambda i, n: (0, 0)),
                  pl.BlockSpec((1, cw), lambda i, n: (0, 0))],
        out_specs=blk,
        out_shape=jax.ShapeDtypeStruct((b, s, cw), BF16),
        compiler_params=_params(2),
        name="gmlp_spatial_gate",
    )(gv, uz, ws, bs_t, v_gain.reshape(1, cw))


def kernel(x, c, norm_g0, ada_w0, ada_b0, w_in0, a_conv_w, a_gate_b, a_norm_g, b_q_gain, b_k_gain, b_rpb, w_out0,
           norm_g1, ada_w1, ada_b1, w_in1, c_v_norm_g, c_w_s, c_b_s, w_out1):
    b, s, d = x.shape
    a_heads, dv = a_norm_g.shape
    a_width = a_heads * dv
    dk = dv // 2
    qk_width = a_heads * dk
    hd = b_q_gain.shape[0]
    b_heads, b_width = b_rpb.shape[0], b_rpb.shape[0] * hd
    n_gate = 4 * a_heads
    a_cols = 2 * qk_width + 3 * a_width
    assert w_in0.shape[1] == a_cols + n_gate + 4 * b_width
    cw = c_v_norm_g.shape[0]
    nc = s // MLSTM_CHUNK

    c_pad = jnp.pad(c, ((0, -b % 8), (0, 0)))
    mod0 = _modulation(c_pad, ada_w0, ada_b0)[:b]
    mod1 = _modulation(c_pad, ada_w1, ada_b1)[:b]
    shift0, scale0, gate0 = mod0[:, :d], mod0[:, d:2 * d], mod0[:, 2 * d:]
    shift1, scale1, gate1 = mod1[:, :d], mod1[:, d:2 * d], mod1[:, 2 * d:]

    h0 = _norm_mod(x, norm_g0, scale0, shift0).reshape(b * s, d)
    qkv_cols = 2 * qk_width + a_width
    wt0 = w_in0.T.astype(BF16)
    qkv = _proj(h0, wt0, [0], qkv_cols, _tile(qkv_cols, 1024), _identity, BF16, "layer0_in_proj_qkv",
                w_transposed=True).reshape(b, s, qkv_cols)
    og = _proj(h0, wt0, [qkv_cols, qkv_cols + a_width], a_width, _tile(a_width, 512), _og_gate, BF16,
               "layer0_in_proj_og", w_transposed=True).reshape(b, s, a_width)
    proj_b = _proj(h0, wt0, [a_cols + n_gate], 4 * b_width, _tile(4 * b_width, 1024), _identity, BF16,
                   "layer0_in_proj_b", w_transposed=True).reshape(b, s, 4 * b_width)
    assert a_cols % LANES == 0 and n_gate <= LANES
    gates = _proj(h0, wt0, [a_cols], LANES, LANES, _identity, F32, "layer0_gate_proj",
                  w_transposed=True)[:, :n_gate]

    q, k, kt = _conv_silu(qkv, a_conv_w, qk_width, dk)
    graw = gates.reshape(b, nc, MLSTM_CHUNK, n_gate).transpose(0, 1, 3, 2)
    ig, pre, suf, mst = _gate_prep(graw, a_gate_b)
    hh = a_heads
    pack = jnp.stack([ig[:, :, :hh], pre[:, :, hh:2 * hh], ig[:, :, 2 * hh:3 * hh], suf[:, :, 3 * hh:],
                      mst[:, :, :hh], mst[:, :, hh:2 * hh], mst[:, :, 2 * hh:3 * hh], mst[:, :, 3 * hh:]],
                     axis=3)
    pack = pack.transpose(0, 2, 1, 3, 4)
    y_a = _mlstm(q, k, kt, qkv, og, pack, a_norm_g, a_heads, dk, dv, 2 * qk_width)

    y_b = _natten(proj_b, b_rpb, b_q_gain, b_k_gain, b_heads, hd, 0)

    assert a_width == b_width
    x1 = _out_proj(y_a, 0, y_b, 0, w_out0.astype(BF16), x, gate0, "layer0_out_proj")

    h1 = _norm_mod(x1, norm_g1, scale1, shift1).reshape(b * s, d)
    w1 = w_in1.astype(BF16)
    uz = _proj(h1, w1, [0, 2 * cw], cw, _tile(cw, 512), _uz_gate, BF16, "gmlp_in_proj_uz")
    gv = _proj(h1, w1, [cw], cw, _tile(cw, 1024), jax.nn.gelu, BF16, "gmlp_in_proj_v")
    y = _spatial(gv.reshape(b, s, cw), uz.reshape(b, s, cw), c_w_s.astype(BF16), c_b_s.T, c_v_norm_g)
    return _out_proj(y, 0, y, 1, w_out1.astype(BF16), x1, gate1, "layer1_out_proj")
```

```python
import functools

import jax
import jax.numpy as jnp
from jax import lax
from jax.experimental import pallas as pl
from jax.experimental.pallas import tpu as pltpu

F32 = jnp.float32
BF16 = jnp.bfloat16
HIGHEST = lax.Precision.HIGHEST

EPS = 1e-6
GRID_W = 64
MLSTM_CHUNK = 256
SPATIAL_CHUNK = 128
LANES = 128
V7X_VMEM_LIMIT_BYTES = 52 * 1024 * 1024


def _params(n_axes, vmem=V7X_VMEM_LIMIT_BYTES):
    return pltpu.CompilerParams(dimension_semantics=("arbitrary",) * n_axes, vmem_limit_bytes=vmem)


def _tile(dim, pref):
    t = min(dim, pref)
    assert dim % t == 0, (dim, pref)
    return t


def _silu(x):
    return x * jax.nn.sigmoid(x)


def _rms(x):
    return x * lax.rsqrt(jnp.mean(x * x, axis=-1, keepdims=True) + EPS)


def _ada_kernel(c_ref, w_ref, b_ref, o_ref):
    a = _silu(c_ref[...]).astype(BF16)
    o_ref[...] = jnp.dot(a, w_ref[...].astype(BF16), preferred_element_type=F32) + b_ref[...]


def _modulation(c_pad, w, b):
    rows, d = c_pad.shape
    n = w.shape[1]
    tn = _tile(n, 512)
    return pl.pallas_call(
        _ada_kernel,
        grid=(n // tn,),
        in_specs=[pl.BlockSpec((rows, d), lambda j: (0, 0)),
                  pl.BlockSpec((d, tn), lambda j: (0, j)),
                  pl.BlockSpec((1, tn), lambda j: (0, j))],
        out_specs=pl.BlockSpec((rows, tn), lambda j: (0, j)),
        out_shape=jax.ShapeDtypeStruct((rows, n), F32),
        compiler_params=_params(1),
        name="ada_modulation",
    )(c_pad, w, b.reshape(1, n))


def _norm_mod_kernel(x_ref, g_ref, sc_ref, sh_ref, o_ref):
    y = _rms(x_ref[0]) * g_ref[...]
    o_ref[0] = (y * (1.0 + sc_ref[0]) + sh_ref[0]).astype(BF16)


def _norm_mod(x, g, scale, shift):
    b, s, d = x.shape
    ts = _tile(s, 256)
    vec = pl.BlockSpec((1, 1, d), lambda i, j: (i, 0, 0))
    return pl.pallas_call(
        _norm_mod_kernel,
        grid=(b, s // ts),
        in_specs=[pl.BlockSpec((1, ts, d), lambda i, j: (i, j, 0)),
                  pl.BlockSpec((1, d), lambda i, j: (0, 0)), vec, vec],
        out_specs=pl.BlockSpec((1, ts, d), lambda i, j: (i, j, 0)),
        out_shape=jax.ShapeDtypeStruct((b, s, d), BF16),
        compiler_params=_params(2),
        name="norm_modulate",
    )(x, g.reshape(1, d), scale.reshape(b, 1, d), shift.reshape(b, 1, d))


BF16_SUBLANES = 16


def _proj_kernel(a_ref, *refs, w_transposed, epilogue):
    w_refs, o_ref = refs[:-1], refs[-1]
    a = a_ref[...]
    dims = (((1,), (1,)), ((), ())) if w_transposed else (((1,), (0,)), ((), ()))
    accs = [lax.dot_general(a, w_ref[...], dims, preferred_element_type=F32) for w_ref in w_refs]
    o_ref[...] = epilogue(*accs).astype(o_ref.dtype)


def _proj(a, w, offsets, n, tn, epilogue, out_dtype, name, w_transposed=False):
    m, k = a.shape
    tm = _tile(m, 1024)
    assert n % tn == 0

    def w_spec(off):
        if not w_transposed:
            assert off % tn == 0
            return pl.BlockSpec((k, tn), lambda i, j: (0, off // tn + j))
        if off % tn == 0:
            return pl.BlockSpec((tn, k), lambda i, j: (off // tn + j, 0))
        sub = BF16_SUBLANES
        assert off % sub == 0 and tn % sub == 0
        return pl.BlockSpec((pl.Element(tn), pl.Element(k)),
                            lambda i, j: ((off // sub + j * (tn // sub)) * sub, 0))

    kern = functools.partial(_proj_kernel, w_transposed=w_transposed, epilogue=epilogue)
    return pl.pallas_call(
        kern,
        grid=(m // tm, n // tn),
        in_specs=[pl.BlockSpec((tm, k), lambda i, j: (i, 0))] + [w_spec(off) for off in offsets],
        out_specs=pl.BlockSpec((tm, tn), lambda i, j: (i, j)),
        out_shape=jax.ShapeDtypeStruct((m, n), out_dtype),
        compiler_params=_params(2),
        name=name,
    )(a, *([w] * len(offsets)))


def _identity(x):
    return x


def _og_gate(o, z):
    return jax.nn.sigmoid(o) * _silu(z)


def _uz_gate(u, z):
    return jax.nn.gelu(u) * _silu(z)


CONV_ROWS = 512
CONV_HALO = BF16_SUBLANES


def _conv3_silu_rows(x_ref, w, r0, rows):
    s = x_ref.shape[1]
    lo, hi = max(r0 - CONV_HALO, 0), min(r0 + rows + CONV_HALO, s)
    x = x_ref[0, lo:hi, :].astype(F32)
    n = hi - lo
    x_prev, x_next = pltpu.roll(x, 1, axis=0), pltpu.roll(x, n - 1, axis=0)
    row = lax.broadcasted_iota(jnp.int32, x.shape, 0)
    if lo == 0:
        x_prev = jnp.where(row == 0, 0.0, x_prev)
    if hi == s:
        x_next = jnp.where(row == n - 1, 0.0, x_next)
    y = _silu(x_prev * w[0:1] + x * w[1:2] + x_next * w[2:3])
    return y[r0 - lo:r0 - lo + rows]


def _conv_kernel(xq_ref, xk_ref, wq_ref, wk_ref, q_ref, k_ref, kt_ref, *, q_scale):
    s = xq_ref.shape[1]
    l = kt_ref.shape[3]
    rows = min(CONV_ROWS, s)
    wq, wk = wq_ref[...], wk_ref[...]
    for r0 in range(0, s, rows):
        q_ref[0, r0:r0 + rows, :] = (_conv3_silu_rows(xq_ref, wq, r0, rows) * q_scale).astype(BF16)
        k = _conv3_silu_rows(xk_ref, wk, r0, rows)
        k_ref[0, r0:r0 + rows, :] = k.astype(BF16)
        for c in range(rows // l):
            kt_ref[0, r0 // l + c] = k[c * l:(c + 1) * l, :].T.astype(BF16)


def _conv_silu(proj, conv_w, qk_width, dk):
    b, s, _ = proj.shape
    assert conv_w.shape[0] == 3, "centred depthwise conv is written for 3 taps"
    assert s % min(CONV_ROWS, s) == 0 and min(CONV_ROWS, s) % MLSTM_CHUNK == 0
    tc = _tile(qk_width, 256)
    nq = qk_width // tc
    l = MLSTM_CHUNK
    kern = functools.partial(_conv_kernel, q_scale=float(dk) ** -0.5)
    col = pl.BlockSpec((1, s, tc), lambda i, j: (i, 0, j))
    return pl.pallas_call(
        kern,
        grid=(b, nq),
        in_specs=[col, pl.BlockSpec((1, s, tc), lambda i, j: (i, 0, nq + j)),
                  pl.BlockSpec((3, tc), lambda i, j: (0, j)), pl.BlockSpec((3, tc), lambda i, j: (0, nq + j))],
        out_specs=[col, col, pl.BlockSpec((1, s // l, tc, l), lambda i, j: (i, 0, j, 0))],
        out_shape=[jax.ShapeDtypeStruct((b, s, qk_width), BF16), jax.ShapeDtypeStruct((b, s, qk_width), BF16),
                   jax.ShapeDtypeStruct((b, s // l, qk_width, l), BF16)],
        compiler_params=_params(2),
        name="qk_conv_silu",
    )(proj, proj, conv_w, conv_w)


def _gates_kernel(x_ref, b_ref, ig_ref, pre_ref, suf_ref, m_ref):
    nc, r, l = x_ref.shape[1:]
    hh = r // 4
    x = x_ref[0] + b_ref[...][None]
    ig_ref[0] = x
    lf = (jnp.minimum(x, 0.0) - jnp.log1p(jnp.exp(-jnp.abs(x)))).reshape(nc * r, l)
    u = lax.broadcasted_iota(jnp.int32, (l, l), 0)
    t = lax.broadcasted_iota(jnp.int32, (l, l), 1)
    pre = jnp.dot(lf, (u <= t).astype(F32), precision=HIGHEST, preferred_element_type=F32).reshape(nc, r, l)
    suf = jnp.dot(lf, (u >= t).astype(F32), precision=HIGHEST, preferred_element_type=F32).reshape(nc, r, l)
    pre_ref[0] = pre
    suf_ref[0] = suf
    g_f, g_b = pre[:, hh:2 * hh], suf[:, 3 * hh:]
    tot_f, tot_b = g_f[:, :, l - 1:l], g_b[:, :, 0:1]
    a_f = jnp.max(tot_f - g_f + x[:, :hh], axis=2, keepdims=True)
    a_b = jnp.max(tot_b - g_b + x[:, 2 * hh:3 * hh], axis=2, keepdims=True)
    m = jnp.zeros((hh, 1), F32)
    for c in range(nc):
        m_ref[0, c, 0:hh] = jnp.broadcast_to(m, (hh, l))
        m = jnp.maximum(tot_f[c] + m, a_f[c])
        m_ref[0, c, hh:2 * hh] = jnp.broadcast_to(m, (hh, l))
    m = jnp.zeros((hh, 1), F32)
    for c in reversed(range(nc)):
        m_ref[0, c, 2 * hh:3 * hh] = jnp.broadcast_to(m, (hh, l))
        m = jnp.maximum(tot_b[c] + m, a_b[c])
        m_ref[0, c, 3 * hh:] = jnp.broadcast_to(m, (hh, l))


def _gate_prep(graw, bias):
    b, nc, r, l = graw.shape
    spec = pl.BlockSpec((1, nc, r, l), lambda i: (i, 0, 0, 0))
    shape = jax.ShapeDtypeStruct(graw.shape, F32)
    return pl.pallas_call(
        _gates_kernel,
        grid=(b,),
        in_specs=[spec, pl.BlockSpec((r, 1), lambda i: (0, 0))],
        out_specs=[spec] * 4,
        out_shape=[shape] * 4,
        compiler_params=_params(1),
        name="mlstm_gate_prep",
    )(graw, bias.reshape(r, 1))


def _row_to_col(row_vec, eye):
    return jnp.sum(jnp.where(eye, row_vec, 0.0), axis=1, keepdims=True)


def _mlstm_local(q, kt, ig, g, tot, m, m_new, *, causal):
    l = q.shape[0]
    r_i = lax.broadcasted_iota(jnp.int32, (l, l), 0)
    c_i = lax.broadcasted_iota(jnp.int32, (l, l), 1)
    g_col = _row_to_col(g, r_i == c_i)
    d = g_col - g + ig
    d = jnp.where((c_i <= r_i) if causal else (c_i >= r_i), d, -jnp.inf)
    gm = g_col + m
    m_t = jnp.maximum(gm, jnp.max(d, axis=1, keepdims=True))
    s_ = jnp.dot(q, kt, preferred_element_type=F32) * jnp.exp(d - m_t)
    w = jnp.exp(tot - g + ig - m_new)
    return dict(
        s=s_.astype(BF16), s_sum=jnp.sum(s_, axis=1, keepdims=True),
        inter=jnp.exp(gm - m_t), floor=jnp.exp(-m_t), decay=jnp.exp(tot + m - m_new),
        wkt=(kt.astype(F32) * w).astype(BF16),
        w_rows=jnp.broadcast_to(w, (8, l)).astype(BF16))


def _mlstm_state(q, k, v, loc, c, n):
    qc = jnp.dot(q, c.astype(BF16), preferred_element_type=F32)
    num = jnp.dot(loc["s"], v, preferred_element_type=F32) + loc["inter"] * qc
    qn = jnp.sum(q.astype(F32) * n, axis=1, keepdims=True)
    den = loc["s_sum"] + loc["inter"] * qn
    h = num * (1.0 / jnp.maximum(jnp.abs(den), loc["floor"]))
    c_new = loc["decay"] * c + jnp.dot(loc["wkt"], v, preferred_element_type=F32)
    n_new = loc["decay"] * n + jnp.dot(loc["w_rows"], k, preferred_element_type=F32)[0:1]
    return h, c_new, n_new


MLSTM_UNROLL = 2


def _mlstm_kernel(q_ref, k_ref, kt_ref, v_ref, og_ref, gp_ref, ng_ref, o_ref, acc_ref, cf_ref, cb_ref):
    s, dv = v_ref.shape[1:]
    dk = q_ref.shape[2]
    l = MLSTM_CHUNK
    nc = s // l
    cf_ref[...] = jnp.zeros_like(cf_ref)
    cb_ref[...] = jnp.zeros_like(cb_ref)
    gain = ng_ref[0]

    def rows_of(c):
        return pl.ds(pl.multiple_of(c * l, l), l)

    def local(c, causal):
        gp = gp_ref[0, 0, c]
        if causal:
            ig, g, m, m_new = gp[0:1], gp[1:2], gp[4:5, 0:1], gp[5:6, 0:1]
            tot = g[:, l - 1:l]
        else:
            ig, g, m, m_new = gp[2:3], gp[3:4], gp[6:7, 0:1], gp[7:8, 0:1]
            tot = g[:, 0:1]
        return _mlstm_local(q_ref[0, rows_of(c), :], kt_ref[0, c], ig, g, tot, m, m_new, causal=causal)

    def emit(rows, h, first_touch):
        if first_touch:
            acc_ref[rows, :] = h
        else:
            y = _rms(acc_ref[rows, :] + h) * gain * og_ref[0, rows, :].astype(F32)
            o_ref[0, rows, :] = y.astype(o_ref.dtype)

    def body(first_touch, i, carry):
        chunks = [(i * MLSTM_UNROLL + u, nc - 1 - (i * MLSTM_UNROLL + u)) for u in range(MLSTM_UNROLL)]
        n_f, n_b = carry
        c_f, c_b = cf_ref[...], cb_ref[...]
        for cf, cb in chunks:
            rows = rows_of(cf)
            h, c_f, n_f = _mlstm_state(q_ref[0, rows, :], k_ref[0, rows, :], v_ref[0, rows, :],
                                       local(cf, True), c_f, n_f)
            emit(rows, h, first_touch)
            rows = rows_of(cb)
            h, c_b, n_b = _mlstm_state(q_ref[0, rows, :], k_ref[0, rows, :], v_ref[0, rows, :],
                                       local(cb, False), c_b, n_b)
            emit(rows, h, first_touch)
        cf_ref[...] = c_f
        cb_ref[...] = c_b
        return n_f, n_b

    n0 = jnp.zeros((1, dk), F32)
    n_it = nc // MLSTM_UNROLL
    carry = lax.fori_loop(0, n_it // 2, functools.partial(body, True), (n0, n0))
    lax.fori_loop(n_it // 2, n_it, functools.partial(body, False), carry)


def _mlstm(q, k, kt, qkv, og, gate_pack, norm_g, heads, dk, dv, v_col0):
    b, s, _ = qkv.shape
    assert s % (2 * MLSTM_UNROLL * MLSTM_CHUNK) == 0 and v_col0 % dv == 0
    nc = s // MLSTM_CHUNK
    seq_dk = pl.BlockSpec((1, s, dk), lambda i, h: (i, 0, h))
    seq_dv = pl.BlockSpec((1, s, dv), lambda i, h: (i, 0, h))
    return pl.pallas_call(
        _mlstm_kernel,
        grid=(b, heads),
        in_specs=[seq_dk, seq_dk,
                  pl.BlockSpec((1, nc, dk, MLSTM_CHUNK), lambda i, h: (i, 0, h, 0)),
                  pl.BlockSpec((1, s, dv), lambda i, h: (i, 0, v_col0 // dv + h)),
                  seq_dv,
                  pl.BlockSpec((1, 1, nc, 8, MLSTM_CHUNK), lambda i, h: (i, h, 0, 0, 0)),
                  pl.BlockSpec((1, 1, dv), lambda i, h: (h, 0, 0))],
        out_specs=seq_dv,
        out_shape=jax.ShapeDtypeStruct((b, s, heads * dv), BF16),
        scratch_shapes=[pltpu.VMEM((s, dv), F32),
                        pltpu.VMEM((dk, dv), F32),
                        pltpu.VMEM((dk, dv), F32)],
        compiler_params=_params(2),
        name="mlstm_bidirectional",
    )(q, k, kt, qkv, og, gate_pack, norm_g.reshape(heads, 1, dv))


def _rpb_toeplitz_kernel(r_ref, o_ref, *, win_c):
    dpad, n = r_ref.shape[1], o_ref.shape[1]
    dd = lax.broadcasted_iota(jnp.int32, (dpad, n), 0)
    cc = lax.broadcasted_iota(jnp.int32, (dpad, n), 1)
    shift = GRID_W.bit_length() - 1
    qc, kc = cc >> shift, cc & (GRID_W - 1)
    dc = jnp.clip(kc - qc + (win_c - 1), 0, 2 * win_c - 2)
    onehot = jnp.where(dd == dc, 1.0, 0.0).astype(F32)
    vals = jnp.dot(r_ref[...], onehot, precision=HIGHEST, preferred_element_type=F32)
    co = lax.broadcasted_iota(jnp.int32, (1, n), 1)
    qo, ko = co >> shift, co & (GRID_W - 1)
    cs = jnp.clip(qo - win_c // 2, 0, GRID_W - win_c)
    o_ref[...] = jnp.where((ko >= cs) & (ko < cs + win_c), vals, -jnp.inf)


def _natten_plan(n_rows, win_r, q_rows, k_rows):
    variants, var_idx, k_start = [], [], []
    for r0 in range(0, n_rows, q_rows):
        kp = min(max(r0 - win_r // 2, 0), n_rows - k_rows)
        pattern = []
        for i in range(q_rows):
            r = r0 + i
            rs = min(max(r - win_r // 2, 0), n_rows - win_r)
            assert kp <= rs and rs + win_r <= kp + k_rows
            pattern.append(tuple((kp + jj - r + win_r - 1) if rs <= kp + jj < rs + win_r else None
                                 for jj in range(k_rows)))
        pattern = tuple(pattern)
        if pattern not in variants:
            variants.append(pattern)
        var_idx.append(variants.index(pattern))
        k_start.append(kp)
    return variants, var_idx, k_start


def _bias_assemble_kernel(t_ref, o_ref, *, variants):
    w = GRID_W
    masked = jnp.full((w, w), -jnp.inf, F32)
    for v, pattern in enumerate(variants):
        for i, row in enumerate(pattern):
            tiles = [masked if dr is None else t_ref[0, dr] for dr in row]
            o_ref[0, v, i * w:(i + 1) * w, :] = jnp.concatenate(tiles, axis=1)


def _natten_bias(rpb, variants):
    hb, n_dr, n_dc = rpb.shape
    win_c = (n_dc + 1) // 2
    w = GRID_W
    dpad = 32
    assert n_dc <= dpad and (hb * n_dr) % 8 == 0
    r = jnp.pad(rpb, ((0, 0), (0, 0), (0, dpad - n_dc))).reshape(hb * n_dr, dpad)
    toeplitz = pl.pallas_call(
        functools.partial(_rpb_toeplitz_kernel, win_c=win_c),
        grid=(1,),
        in_specs=[pl.BlockSpec(r.shape, lambda i: (0, 0))],
        out_specs=pl.BlockSpec((r.shape[0], w * w), lambda i: (0, 0)),
        out_shape=jax.ShapeDtypeStruct((r.shape[0], w * w), F32),
        compiler_params=_params(1),
        name="natten_rpb_toeplitz",
    )(r).reshape(hb, n_dr, w, w)
    q_rows, k_rows = len(variants[0]), len(variants[0][0])
    out_blk = (1, len(variants), q_rows * w, k_rows * w)
    return pl.pallas_call(
        functools.partial(_bias_assemble_kernel, variants=variants),
        grid=(hb,),
        in_specs=[pl.BlockSpec((1, n_dr, w, w), lambda h: (h, 0, 0, 0))],
        out_specs=pl.BlockSpec(out_blk, lambda h: (h, 0, 0, 0)),
        out_shape=jax.ShapeDtypeStruct((hb,) + out_blk[1:], F32),
        compiler_params=_params(1),
        name="natten_bias_assemble",
    )(toeplitz)


NATTEN_Q_ROWS = 2
NATTEN_BATCH = 4


def _natten_kernel(var_ref, ks_ref, q_ref, k_ref, v_ref, z_ref, bias_ref, qg_ref, kg_ref, o_ref,
                   qn_ref, kn_ref, sc_a, sc_b):
    s, hd = q_ref.shape[1:]
    w = GRID_W
    nq, nk = bias_ref.shape[2:]
    n_batches = s // (nq * NATTEN_BATCH)
    qn_ref[...] = (_rms(q_ref[0].astype(F32)) * qg_ref[...] * (float(hd) ** -0.5)).astype(BF16)
    kn_ref[...] = (_rms(k_ref[0].astype(F32)) * kg_ref[...]).astype(BF16)

    def rows_of(p):
        return (pl.ds(pl.multiple_of(p * nq, nq), nq), pl.ds(pl.multiple_of(ks_ref[p] * w, w), nk))

    def scores(p):
        q_rows, kv_rows = rows_of(p)
        qk = lax.dot_general(qn_ref[q_rows, :], kn_ref[kv_rows, :], (((1,), (1,)), ((), ())),
                             preferred_element_type=F32)
        return qk + bias_ref[0, var_ref[p]]

    def attend(p, sc):
        q_rows, kv_rows = rows_of(p)
        p_un = jnp.exp(sc - jnp.max(sc, axis=1, keepdims=True))
        o = jnp.dot(p_un.astype(BF16), v_ref[0, kv_rows, :], preferred_element_type=F32)
        o = o * (1.0 / jnp.sum(p_un, axis=1, keepdims=True))
        o_ref[0, q_rows, :] = (o * _silu(z_ref[0, q_rows, :].astype(F32))).astype(o_ref.dtype)

    def fill(buf, j):
        for u in range(NATTEN_BATCH):
            buf[u] = scores(j * NATTEN_BATCH + u)

    def drain(buf, j):
        for u in range(NATTEN_BATCH):
            attend(j * NATTEN_BATCH + u, buf[u])

    def body(k, carry):
        fill(sc_b, 2 * k + 1)
        drain(sc_a, 2 * k)
        fill(sc_a, 2 * k + 2)
        drain(sc_b, 2 * k + 1)
        return carry

    fill(sc_a, 0)
    lax.fori_loop(0, n_batches // 2 - 1, body, 0)
    fill(sc_b, n_batches - 1)
    drain(sc_a, n_batches - 2)
    drain(sc_b, n_batches - 1)


def _natten(proj, rpb, q_gain, k_gain, heads, hd, col0):
    b, s, _ = proj.shape
    win_r = (rpb.shape[1] + 1) // 2
    n_rows = s // GRID_W
    k_rows = win_r + NATTEN_Q_ROWS
    assert s % GRID_W == 0 and n_rows >= k_rows and col0 % hd == 0
    assert n_rows % (2 * NATTEN_Q_ROWS * NATTEN_BATCH) == 0
    variants, var_idx, k_start = _natten_plan(n_rows, win_r, NATTEN_Q_ROWS, k_rows)
    bias = _natten_bias(rpb, variants)
    sc_buf = pltpu.VMEM((NATTEN_BATCH,) + bias.shape[2:], F32)
    c0 = col0 // hd

    def col(part):
        return pl.BlockSpec((1, s, hd), lambda h, i: (i, 0, c0 + part * heads + h))

    smem = pl.BlockSpec(memory_space=pltpu.SMEM)
    vec = pl.BlockSpec((1, hd), lambda h, i: (0, 0))
    return pl.pallas_call(
        _natten_kernel,
        grid=(heads, b),
        in_specs=[smem, smem, col(0), col(1), col(2), col(3),
                  pl.BlockSpec((1,) + bias.shape[1:], lambda h, i: (h, 0, 0, 0)), vec, vec],
        out_specs=pl.BlockSpec((1, s, hd), lambda h, i: (i, 0, h)),
        out_shape=jax.ShapeDtypeStruct((b, s, heads * hd), BF16),
        scratch_shapes=[pltpu.VMEM((s, hd), BF16), pltpu.VMEM((s, hd), BF16), sc_buf, sc_buf],
        compiler_params=_params(2),
        name="neighbourhood_attention",
    )(jnp.asarray(var_idx, jnp.int32), jnp.asarray(k_start, jnp.int32), proj, proj, proj, proj, bias,
      q_gain.reshape(1, hd), k_gain.reshape(1, hd))


def _mm_out_kernel(a1_ref, a2_ref, w1_ref, w2_ref, x_ref, gate_ref, o_ref):
    acc = jnp.dot(a1_ref[0], w1_ref[...], preferred_element_type=F32)
    acc = acc + jnp.dot(a2_ref[0], w2_ref[...], preferred_element_type=F32)
    o_ref[0] = x_ref[0] + gate_ref[0] * acc


def _out_proj(a1, a1_blk, a2, a2_blk, w, x, gate, name):
    b, s, d = x.shape
    kh = w.shape[0] // 2
    tm, tn = _tile(s, 512), _tile(d, 512)
    return pl.pallas_call(
        _mm_out_kernel,
        grid=(b, s // tm, d // tn),
        in_specs=[pl.BlockSpec((1, tm, kh), lambda i, m, n: (i, m, a1_blk)),
                  pl.BlockSpec((1, tm, kh), lambda i, m, n: (i, m, a2_blk)),
                  pl.BlockSpec((kh, tn), lambda i, m, n: (0, n)),
                  pl.BlockSpec((kh, tn), lambda i, m, n: (1, n)),
                  pl.BlockSpec((1, tm, tn), lambda i, m, n: (i, m, n)),
                  pl.BlockSpec((1, 1, tn), lambda i, m, n: (i, 0, n))],
        out_specs=pl.BlockSpec((1, tm, tn), lambda i, m, n: (i, m, n)),
        out_shape=jax.ShapeDtypeStruct((b, s, d), F32),
        compiler_params=_params(3),
        name=name,
    )(a1, a2, w, w, x, gate.reshape(b, 1, d))


def _spatial_kernel(gv_ref, uz_ref, ws_ref, bs_ref, vg_ref, o_ref):
    groups = ws_ref.shape[0]
    cg = gv_ref.shape[2] // groups
    vn = (_rms(gv_ref[0].astype(F32)) * vg_ref[...]).astype(BF16)
    for g in range(groups):
        cols = slice(g * cg, (g + 1) * cg)
        sv = jnp.dot(ws_ref[g], vn[:, cols], preferred_element_type=F32) + bs_ref[:, g:g + 1]
        o_ref[0, :, cols] = (uz_ref[0, :, cols].astype(F32) * sv).astype(BF16)


def _spatial(gv, uz, ws, bs_t, v_gain):
    b, s, cw = gv.shape
    l = SPATIAL_CHUNK
    groups = ws.shape[0]
    assert s % l == 0 and ws.shape[1:] == (l, l)
    blk = pl.BlockSpec((1, l, cw), lambda i, n: (i, n, 0))
    return pl.pallas_call(
        _spatial_kernel,
        grid=(b, s // l),
        in_specs=[blk, blk,
                  pl.BlockSpec(ws.shape, lambda i, n: (0, 0, 0)),
                  pl.BlockSpec((l, groups), lambda i, n: (0, 0)),
                  pl.BlockSpec((1, cw), lambda i, n: (0, 0))],
        out_specs=blk,
        out_shape=jax.ShapeDtypeStruct((b, s, cw), BF16),
        compiler_params=_params(2),
        name="gmlp_spatial_gate",
    )(gv, uz, ws, bs_t, v_gain.reshape(1, cw))


def kernel(x, c, norm_g0, ada_w0, ada_b0, w_in0, a_conv_w, a_gate_b, a_norm_g, b_q_gain, b_k_gain, b_rpb, w_out0,
           norm_g1, ada_w1, ada_b1, w_in1, c_v_norm_g, c_w_s, c_b_s, w_out1):
    b, s, d = x.shape
    a_heads, dv = a_norm_g.shape
    a_width = a_heads * dv
    dk = dv // 2
    qk_width = a_heads * dk
    hd = b_q_gain.shape[0]
    b_heads, b_width = b_rpb.shape[0], b_rpb.shape[0] * hd
    n_gate = 4 * a_heads
    a_cols = 2 * qk_width + 3 * a_width
    assert w_in0.shape[1] == a_cols + n_gate + 4 * b_width
    cw = c_v_norm_g.shape[0]
    nc = s // MLSTM_CHUNK

    c_pad = jnp.pad(c, ((0, -b % 8), (0, 0)))
    mod0 = _modulation(c_pad, ada_w0, ada_b0)[:b]
    mod1 = _modulation(c_pad, ada_w1, ada_b1)[:b]
    shift0, scale0, gate0 = mod0[:, :d], mod0[:, d:2 * d], mod0[:, 2 * d:]
    shift1, scale1, gate1 = mod1[:, :d], mod1[:, d:2 * d], mod1[:, 2 * d:]

    h0 = _norm_mod(x, norm_g0, scale0, shift0).reshape(b * s, d)
    qkv_cols = 2 * qk_width + a_width
    wt0 = w_in0.T.astype(BF16)
    qkv = _proj(h0, wt0, [0], qkv_cols, _tile(qkv_cols, 1024), _identity, BF16, "layer0_in_proj_qkv",
                w_transposed=True).reshape(b, s, qkv_cols)
    og = _proj(h0, wt0, [qkv_cols, qkv_cols + a_width], a_width, _tile(a_width, 512), _og_gate, BF16,
               "layer0_in_proj_og", w_transposed=True).reshape(b, s, a_width)
    proj_b = _proj(h0, wt0, [a_cols + n_gate], 4 * b_width, _tile(4 * b_width, 1024), _identity, BF16,
                   "layer0_in_proj_b", w_transposed=True).reshape(b, s, 4 * b_width)
    assert a_cols % LANES == 0 and n_gate <= LANES
    gates = _proj(h0, wt0, [a_cols], LANES, LANES, _identity, F32, "layer0_gate_proj",
                  w_transposed=True)[:, :n_gate]

    q, k, kt = _conv_silu(qkv, a_conv_w, qk_width, dk)
    graw = gates.reshape(b, nc, MLSTM_CHUNK, n_gate).transpose(0, 1, 3, 2)
    ig, pre, suf, mst = _gate_prep(graw, a_gate_b)
    hh = a_heads
    pack = jnp.stack([ig[:, :, :hh], pre[:, :, hh:2 * hh], ig[:, :, 2 * hh:3 * hh], suf[:, :, 3 * hh:],
                      mst[:, :, :hh], mst[:, :, hh:2 * hh], mst[:, :, 2 * hh:3 * hh], mst[:, :, 3 * hh:]],
                     axis=3)
    pack = pack.transpose(0, 2, 1, 3, 4)
    y_a = _mlstm(q, k, kt, qkv, og, pack, a_norm_g, a_heads, dk, dv, 2 * qk_width)

    y_b = _natten(proj_b, b_rpb, b_q_gain, b_k_gain, b_heads, hd, 0)

    assert a_width == b_width
    x1 = _out_proj(y_a, 0, y_b, 0, w_out0.astype(BF16), x, gate0, "layer0_out_proj")

    h1 = _norm_mod(x1, norm_g1, scale1, shift1).reshape(b * s, d)
    w1 = w_in1.astype(BF16)
    uz = _proj(h1, w1, [0, 2 * cw], cw, _tile(cw, 512), _uz_gate, BF16, "gmlp_in_proj_uz")
    gv = _proj(h1, w1, [cw], cw, _tile(cw, 1024), jax.nn.gelu, BF16, "gmlp_in_proj_v")
    y = _spatial(gv.reshape(b, s, cw), uz.reshape(b, s, cw), c_w_s.astype(BF16), c_b_s.T, c_v_norm_g)
    return _out_proj(y, 0, y, 1, w_out1.astype(BF16), x1, gate1, "layer1_out_proj")
```

```python
import functools

import jax
import jax.numpy as jnp
from jax import lax
from jax.experimental import pallas as pl
from jax.experimental.pallas import tpu as pltpu

F32 = jnp.float32
BF16 = jnp.bfloat16
HIGHEST = lax.Precision.HIGHEST

EPS = 1e-6
GRID_W = 64
MLSTM_CHUNK = 256
SPATIAL_CHUNK = 128
LANES = 128
V7X_VMEM_LIMIT_BYTES = 52 * 1024 * 1024


def _params(n_axes, vmem=V7X_VMEM_LIMIT_BYTES):
    return pltpu.CompilerParams(dimension_semantics=("arbitrary",) * n_axes, vmem_limit_bytes=vmem)


def _tile(dim, pref):
    t = min(dim, pref)
    assert dim % t == 0, (dim, pref)
    return t


def _silu(x):
    return x * jax.nn.sigmoid(x)


def _rms(x):
    return x * lax.rsqrt(jnp.mean(x * x, axis=-1, keepdims=True) + EPS)


def _ada_kernel(c_ref, w_ref, b_ref, o_ref):
    a = _silu(c_ref[...]).astype(BF16)
    o_ref[...] = jnp.dot(a, w_ref[...].astype(BF16), preferred_element_type=F32) + b_ref[...]


def _modulation(c_pad, w, b):
    rows, d = c_pad.shape
    n = w.shape[1]
    tn = _tile(n, 512)
    return pl.pallas_call(
        _ada_kernel,
        grid=(n // tn,),
        in_specs=[pl.BlockSpec((rows, d), lambda j: (0, 0)),
                  pl.BlockSpec((d, tn), lambda j: (0, j)),
                  pl.BlockSpec((1, tn), lambda j: (0, j))],
        out_specs=pl.BlockSpec((rows, tn), lambda j: (0, j)),
        out_shape=jax.ShapeDtypeStruct((rows, n), F32),
        compiler_params=_params(1),
        name="ada_modulation",
    )(c_pad, w, b.reshape(1, n))


def _norm_mod_kernel(x_ref, g_ref, sc_ref, sh_ref, o_ref):
    y = _rms(x_ref[0]) * g_ref[...]
    o_ref[0] = (y * (1.0 + sc_ref[0]) + sh_ref[0]).astype(BF16)


def _norm_mod(x, g, scale, shift):
    b, s, d = x.shape
    ts = _tile(s, 256)
    vec = pl.BlockSpec((1, 1, d), lambda i, j: (i, 0, 0))
    return pl.pallas_call(
        _norm_mod_kernel,
        grid=(b, s // ts),
        in_specs=[pl.BlockSpec((1, ts, d), lambda i, j: (i, j, 0)),
                  pl.BlockSpec((1, d), lambda i, j: (0, 0)), vec, vec],
        out_specs=pl.BlockSpec((1, ts, d), lambda i, j: (i, j, 0)),
        out_shape=jax.ShapeDtypeStruct((b, s, d), BF16),
        compiler_params=_params(2),
        name="norm_modulate",
    )(x, g.reshape(1, d), scale.reshape(b, 1, d), shift.reshape(b, 1, d))


BF16_SUBLANES = 16


F32_SUBLANES = 8


def _dot_dims(w_transposed):
    return (((1,), (1,)), ((), ())) if w_transposed else (((1,), (0,)), ((), ()))


def _proj_head_kernel(a_ref, *refs, n_w, w_transposed, epilogue):
    w_refs, o_ref, wb_refs = refs[:n_w], refs[n_w], refs[n_w + 1:]
    a = a_ref[...]
    accs = []
    for w_ref, wb_ref in zip(w_refs, wb_refs):
        wb = w_ref[...].astype(BF16)
        wb_ref[...] = wb
        accs.append(lax.dot_general(a, wb, _dot_dims(w_transposed), preferred_element_type=F32))
    o_ref[...] = epilogue(*accs).astype(o_ref.dtype)


def _proj_tail_kernel(a_ref, *refs, n_w, w_transposed, epilogue):
    w_refs, o_ref = refs[:n_w], refs[-1]
    a = a_ref[...]
    accs = [lax.dot_general(a, w_ref[...], _dot_dims(w_transposed), preferred_element_type=F32)
            for w_ref in w_refs]
    o_ref[...] = epilogue(*accs).astype(o_ref.dtype)


def _proj(a, w, offsets, n, tn, epilogue, out_dtype, name, w_transposed=False):
    m, k = a.shape
    n_w = len(offsets)
    tm = _tile(m, 1024)
    assert n % tn == 0 and m % tm == 0
    tn_head = _tile(tn, 512 // n_w)

    def head_w_spec(off):
        if not w_transposed:
            assert off % tn_head == 0
            return pl.BlockSpec((k, tn_head), lambda j: (0, off // tn_head + j))
        sub = F32_SUBLANES
        assert off % sub == 0 and tn_head % sub == 0
        return pl.BlockSpec((pl.Element(tn_head), pl.Element(k)),
                            lambda j: ((off // sub + j * (tn_head // sub)) * sub, 0))

    if w_transposed:
        wb_shape, wb_head, wb_tail = (n, k), pl.BlockSpec((tn_head, k), lambda j: (j, 0)), \
            pl.BlockSpec((tn, k), lambda i, j: (j, 0))
    else:
        wb_shape, wb_head, wb_tail = (k, n), pl.BlockSpec((k, tn_head), lambda j: (0, j)), \
            pl.BlockSpec((k, tn), lambda i, j: (0, j))
    out_shape = jax.ShapeDtypeStruct((m, n), out_dtype)
    head = pl.pallas_call(
        functools.partial(_proj_head_kernel, n_w=n_w, w_transposed=w_transposed, epilogue=epilogue),
        grid=(n // tn_head,),
        in_specs=[pl.BlockSpec((tm, k), lambda j: (0, 0))] + [head_w_spec(off) for off in offsets],
        out_specs=[pl.BlockSpec((tm, tn_head), lambda j: (0, j))] + [wb_head] * n_w,
        out_shape=[out_shape] + [jax.ShapeDtypeStruct(wb_shape, BF16)] * n_w,
        compiler_params=_params(1),
        name=name + "_head",
    )(a, *([w] * n_w))
    out, wbs = head[0], head[1:]
    if m == tm:
        return out
    return pl.pallas_call(
        functools.partial(_proj_tail_kernel, n_w=n_w, w_transposed=w_transposed, epilogue=epilogue),
        grid=(m // tm - 1, n // tn),
        in_specs=[pl.BlockSpec((tm, k), lambda i, j: (i + 1, 0))] + [wb_tail] * n_w
        + [pl.BlockSpec(memory_space=pl.ANY)],
        out_specs=pl.BlockSpec((tm, tn), lambda i, j: (i + 1, j)),
        out_shape=out_shape,
        input_output_aliases={1 + n_w: 0},
        compiler_params=_params(2),
        name=name,
    )(a, *wbs, out)


def _identity(x):
    return x


def _og_gate(o, z):
    return jax.nn.sigmoid(o) * _silu(z)


def _uz_gate(u, z):
    return jax.nn.gelu(u) * _silu(z)


CONV_ROWS = 512
CONV_HALO = BF16_SUBLANES


def _conv3_silu_rows(x_ref, w, r0, rows):
    s = x_ref.shape[1]
    lo, hi = max(r0 - CONV_HALO, 0), min(r0 + rows + CONV_HALO, s)
    x = x_ref[0, lo:hi, :].astype(F32)
    n = hi - lo
    x_prev, x_next = pltpu.roll(x, 1, axis=0), pltpu.roll(x, n - 1, axis=0)
    row = lax.broadcasted_iota(jnp.int32, x.shape, 0)
    if lo == 0:
        x_prev = jnp.where(row == 0, 0.0, x_prev)
    if hi == s:
        x_next = jnp.where(row == n - 1, 0.0, x_next)
    y = _silu(x_prev * w[0:1] + x * w[1:2] + x_next * w[2:3])
    return y[r0 - lo:r0 - lo + rows]


def _conv_kernel(xq_ref, xk_ref, wq_ref, wk_ref, q_ref, k_ref, kt_ref, *, q_scale):
    s = xq_ref.shape[1]
    l = kt_ref.shape[3]
    rows = min(CONV_ROWS, s)
    wq, wk = wq_ref[...], wk_ref[...]
    for r0 in range(0, s, rows):
        q_ref[0, r0:r0 + rows, :] = (_conv3_silu_rows(xq_ref, wq, r0, rows) * q_scale).astype(BF16)
        k = _conv3_silu_rows(xk_ref, wk, r0, rows)
        k_ref[0, r0:r0 + rows, :] = k.astype(BF16)
        for c in range(rows // l):
            kt_ref[0, r0 // l + c] = k[c * l:(c + 1) * l, :].T.astype(BF16)


def _conv_silu(proj, conv_w, qk_width, dk):
    b, s, _ = proj.shape
    assert conv_w.shape[0] == 3, "centred depthwise conv is written for 3 taps"
    assert s % min(CONV_ROWS, s) == 0 and min(CONV_ROWS, s) % MLSTM_CHUNK == 0
    tc = _tile(qk_width, 256)
    nq = qk_width // tc
    l = MLSTM_CHUNK
    kern = functools.partial(_conv_kernel, q_scale=float(dk) ** -0.5)
    col = pl.BlockSpec((1, s, tc), lambda i, j: (i, 0, j))
    return pl.pallas_call(
        kern,
        grid=(b, nq),
        in_specs=[col, pl.BlockSpec((1, s, tc), lambda i, j: (i, 0, nq + j)),
                  pl.BlockSpec((3, tc), lambda i, j: (0, j)), pl.BlockSpec((3, tc), lambda i, j: (0, nq + j))],
        out_specs=[col, col, pl.BlockSpec((1, s // l, tc, l), lambda i, j: (i, 0, j, 0))],
        out_shape=[jax.ShapeDtypeStruct((b, s, qk_width), BF16), jax.ShapeDtypeStruct((b, s, qk_width), BF16),
                   jax.ShapeDtypeStruct((b, s // l, qk_width, l), BF16)],
        compiler_params=_params(2),
        name="qk_conv_silu",
    )(proj, proj, conv_w, conv_w)


def _gates_kernel(x_ref, b_ref, ig_ref, pre_ref, suf_ref, m_ref):
    nc, r, l = x_ref.shape[1:]
    hh = r // 4
    x = x_ref[0] + b_ref[...][None]
    ig_ref[0] = x
    lf = (jnp.minimum(x, 0.0) - jnp.log1p(jnp.exp(-jnp.abs(x)))).reshape(nc * r, l)
    u = lax.broadcasted_iota(jnp.int32, (l, l), 0)
    t = lax.broadcasted_iota(jnp.int32, (l, l), 1)
    pre = jnp.dot(lf, (u <= t).astype(F32), precision=HIGHEST, preferred_element_type=F32).reshape(nc, r, l)
    suf = jnp.dot(lf, (u >= t).astype(F32), precision=HIGHEST, preferred_element_type=F32).reshape(nc, r, l)
    pre_ref[0] = pre
    suf_ref[0] = suf
    g_f, g_b = pre[:, hh:2 * hh], suf[:, 3 * hh:]
    tot_f, tot_b = g_f[:, :, l - 1:l], g_b[:, :, 0:1]
    a_f = jnp.max(tot_f - g_f + x[:, :hh], axis=2, keepdims=True)
    a_b = jnp.max(tot_b - g_b + x[:, 2 * hh:3 * hh], axis=2, keepdims=True)
    m = jnp.zeros((hh, 1), F32)
    for c in range(nc):
        m_ref[0, c, 0:hh] = jnp.broadcast_to(m, (hh, l))
        m = jnp.maximum(tot_f[c] + m, a_f[c])
        m_ref[0, c, hh:2 * hh] = jnp.broadcast_to(m, (hh, l))
    m = jnp.zeros((hh, 1), F32)
    for c in reversed(range(nc)):
        m_ref[0, c, 2 * hh:3 * hh] = jnp.broadcast_to(m, (hh, l))
        m = jnp.maximum(tot_b[c] + m, a_b[c])
        m_ref[0, c, 3 * hh:] = jnp.broadcast_to(m, (hh, l))


def _gate_prep(graw, bias):
    b, nc, r, l = graw.shape
    spec = pl.BlockSpec((1, nc, r, l), lambda i: (i, 0, 0, 0))
    shape = jax.ShapeDtypeStruct(graw.shape, F32)
    return pl.pallas_call(
        _gates_kernel,
        grid=(b,),
        in_specs=[spec, pl.BlockSpec((r, 1), lambda i: (0, 0))],
        out_specs=[spec] * 4,
        out_shape=[shape] * 4,
        compiler_params=_params(1),
        name="mlstm_gate_prep",
    )(graw, bias.reshape(r, 1))


def _row_to_col(row_vec, eye):
    return jnp.sum(jnp.where(eye, row_vec, 0.0), axis=1, keepdims=True)


def _mlstm_local(q, kt, ig, g, tot, m, m_new, *, causal):
    l = q.shape[0]
    r_i = lax.broadcasted_iota(jnp.int32, (l, l), 0)
    c_i = lax.broadcasted_iota(jnp.int32, (l, l), 1)
    g_col = _row_to_col(g, r_i == c_i)
    d = g_col - g + ig
    d = jnp.where((c_i <= r_i) if causal else (c_i >= r_i), d, -jnp.inf)
    gm = g_col + m
    m_t = jnp.maximum(gm, jnp.max(d, axis=1, keepdims=True))
    s_ = jnp.dot(q, kt, preferred_element_type=F32) * jnp.exp(d - m_t)
    w = jnp.exp(tot - g + ig - m_new)
    return dict(
        s=s_.astype(BF16), s_sum=jnp.sum(s_, axis=1, keepdims=True),
        inter=jnp.exp(gm - m_t), floor=jnp.exp(-m_t), decay=jnp.exp(tot + m - m_new),
        wkt=(kt.astype(F32) * w).astype(BF16),
        w_rows=jnp.broadcast_to(w, (8, l)).astype(BF16))


def _mlstm_state(q, k, v, loc, c, n):
    qc = jnp.dot(q, c.astype(BF16), preferred_element_type=F32)
    num = jnp.dot(loc["s"], v, preferred_element_type=F32) + loc["inter"] * qc
    qn = jnp.sum(q.astype(F32) * n, axis=1, keepdims=True)
    den = loc["s_sum"] + loc["inter"] * qn
    h = num * (1.0 / jnp.maximum(jnp.abs(den), loc["floor"]))
    c_new = loc["decay"] * c + jnp.dot(loc["wkt"], v, preferred_element_type=F32)
    n_new = loc["decay"] * n + jnp.dot(loc["w_rows"], k, preferred_element_type=F32)[0:1]
    return h, c_new, n_new


MLSTM_UNROLL = 2


def _mlstm_kernel(q_ref, k_ref, kt_ref, v_ref, og_ref, gp_ref, ng_ref, o_ref, acc_ref, cf_ref, cb_ref):
    s, dv = v_ref.shape[1:]
    dk = q_ref.shape[2]
    l = MLSTM_CHUNK
    nc = s // l
    cf_ref[...] = jnp.zeros_like(cf_ref)
    cb_ref[...] = jnp.zeros_like(cb_ref)
    gain = ng_ref[0]

    def rows_of(c):
        return pl.ds(pl.multiple_of(c * l, l), l)

    def local(c, causal):
        gp = gp_ref[0, 0, c]
        if causal:
            ig, g, m, m_new = gp[0:1], gp[1:2], gp[4:5, 0:1], gp[5:6, 0:1]
            tot = g[:, l - 1:l]
        else:
            ig, g, m, m_new = gp[2:3], gp[3:4], gp[6:7, 0:1], gp[7:8, 0:1]
            tot = g[:, 0:1]
        return _mlstm_local(q_ref[0, rows_of(c), :], kt_ref[0, c], ig, g, tot, m, m_new, causal=causal)

    def emit(rows, h, first_touch):
        if first_touch:
            acc_ref[rows, :] = h
        else:
            y = _rms(acc_ref[rows, :] + h) * gain * og_ref[0, rows, :].astype(F32)
            o_ref[0, rows, :] = y.astype(o_ref.dtype)

    def body(first_touch, i, carry):
        chunks = [(i * MLSTM_UNROLL + u, nc - 1 - (i * MLSTM_UNROLL + u)) for u in range(MLSTM_UNROLL)]
        n_f, n_b = carry
        c_f, c_b = cf_ref[...], cb_ref[...]
        for cf, cb in chunks:
            rows = rows_of(cf)
            h, c_f, n_f = _mlstm_state(q_ref[0, rows, :], k_ref[0, rows, :], v_ref[0, rows, :],
                                       local(cf, True), c_f, n_f)
            emit(rows, h, first_touch)
            rows = rows_of(cb)
            h, c_b, n_b = _mlstm_state(q_ref[0, rows, :], k_ref[0, rows, :], v_ref[0, rows, :],
                                       local(cb, False), c_b, n_b)
            emit(rows, h, first_touch)
        cf_ref[...] = c_f
        cb_ref[...] = c_b
        return n_f, n_b

    n0 = jnp.zeros((1, dk), F32)
    n_it = nc // MLSTM_UNROLL
    carry = lax.fori_loop(0, n_it // 2, functools.partial(body, True), (n0, n0))
    lax.fori_loop(n_it // 2, n_it, functools.partial(body, False), carry)


def _mlstm(q, k, kt, qkv, og, gate_pack, norm_g, heads, dk, dv, v_col0):
    b, s, _ = qkv.shape
    assert s % (2 * MLSTM_UNROLL * MLSTM_CHUNK) == 0 and v_col0 % dv == 0
    nc = s // MLSTM_CHUNK
    seq_dk = pl.BlockSpec((1, s, dk), lambda i, h: (i, 0, h))
    seq_dv = pl.BlockSpec((1, s, dv), lambda i, h: (i, 0, h))
    return pl.pallas_call(
        _mlstm_kernel,
        grid=(b, heads),
        in_specs=[seq_dk, seq_dk,
                  pl.BlockSpec((1, nc, dk, MLSTM_CHUNK), lambda i, h: (i, 0, h, 0)),
                  pl.BlockSpec((1, s, dv), lambda i, h: (i, 0, v_col0 // dv + h)),
                  seq_dv,
                  pl.BlockSpec((1, 1, nc, 8, MLSTM_CHUNK), lambda i, h: (i, h, 0, 0, 0)),
                  pl.BlockSpec((1, 1, dv), lambda i, h: (h, 0, 0))],
        out_specs=seq_dv,
        out_shape=jax.ShapeDtypeStruct((b, s, heads * dv), BF16),
        scratch_shapes=[pltpu.VMEM((s, dv), F32),
                        pltpu.VMEM((dk, dv), F32),
                        pltpu.VMEM((dk, dv), F32)],
        compiler_params=_params(2),
        name="mlstm_bidirectional",
    )(q, k, kt, qkv, og, gate_pack, norm_g.reshape(heads, 1, dv))


def _rpb_toeplitz_kernel(r_ref, o_ref, *, win_c):
    dpad, n = r_ref.shape[1], o_ref.shape[1]
    dd = lax.broadcasted_iota(jnp.int32, (dpad, n), 0)
    cc = lax.broadcasted_iota(jnp.int32, (dpad, n), 1)
    shift = GRID_W.bit_length() - 1
    qc, kc = cc >> shift, cc & (GRID_W - 1)
    dc = jnp.clip(kc - qc + (win_c - 1), 0, 2 * win_c - 2)
    onehot = jnp.where(dd == dc, 1.0, 0.0).astype(F32)
    vals = jnp.dot(r_ref[...], onehot, precision=HIGHEST, preferred_element_type=F32)
    co = lax.broadcasted_iota(jnp.int32, (1, n), 1)
    qo, ko = co >> shift, co & (GRID_W - 1)
    cs = jnp.clip(qo - win_c // 2, 0, GRID_W - win_c)
    o_ref[...] = jnp.where((ko >= cs) & (ko < cs + win_c), vals, -jnp.inf)


def _natten_plan(n_rows, win_r, q_rows, k_rows):
    variants, var_idx, k_start = [], [], []
    for r0 in range(0, n_rows, q_rows):
        kp = min(max(r0 - win_r // 2, 0), n_rows - k_rows)
        pattern = []
        for i in range(q_rows):
            r = r0 + i
            rs = min(max(r - win_r // 2, 0), n_rows - win_r)
            assert kp <= rs and rs + win_r <= kp + k_rows
            pattern.append(tuple((kp + jj - r + win_r - 1) if rs <= kp + jj < rs + win_r else None
                                 for jj in range(k_rows)))
        pattern = tuple(pattern)
        if pattern not in variants:
            variants.append(pattern)
        var_idx.append(variants.index(pattern))
        k_start.append(kp)
    return variants, var_idx, k_start


def _bias_assemble_kernel(t_ref, o_ref, *, variants):
    w = GRID_W
    masked = jnp.full((w, w), -jnp.inf, F32)
    for v, pattern in enumerate(variants):
        for i, row in enumerate(pattern):
            tiles = [masked if dr is None else t_ref[0, dr] for dr in row]
            o_ref[0, v, i * w:(i + 1) * w, :] = jnp.concatenate(tiles, axis=1)


def _natten_bias(rpb, variants):
    hb, n_dr, n_dc = rpb.shape
    win_c = (n_dc + 1) // 2
    w = GRID_W
    dpad = 32
    assert n_dc <= dpad and (hb * n_dr) % 8 == 0
    r = jnp.pad(rpb, ((0, 0), (0, 0), (0, dpad - n_dc))).reshape(hb * n_dr, dpad)
    toeplitz = pl.pallas_call(
        functools.partial(_rpb_toeplitz_kernel, win_c=win_c),
        grid=(1,),
        in_specs=[pl.BlockSpec(r.shape, lambda i: (0, 0))],
        out_specs=pl.BlockSpec((r.shape[0], w * w), lambda i: (0, 0)),
        out_shape=jax.ShapeDtypeStruct((r.shape[0], w * w), F32),
        compiler_params=_params(1),
        name="natten_rpb_toeplitz",
    )(r).reshape(hb, n_dr, w, w)
    q_rows, k_rows = len(variants[0]), len(variants[0][0])
    out_blk = (1, len(variants), q_rows * w, k_rows * w)
    return pl.pallas_call(
        functools.partial(_bias_assemble_kernel, variants=variants),
        grid=(hb,),
        in_specs=[pl.BlockSpec((1, n_dr, w, w), lambda h: (h, 0, 0, 0))],
        out_specs=pl.BlockSpec(out_blk, lambda h: (h, 0, 0, 0)),
        out_shape=jax.ShapeDtypeStruct((hb,) + out_blk[1:], F32),
        compiler_params=_params(1),
        name="natten_bias_assemble",
    )(toeplitz)


NATTEN_Q_ROWS = 2
NATTEN_BATCH = 4


def _natten_kernel(var_ref, ks_ref, q_ref, k_ref, v_ref, z_ref, bias_ref, qg_ref, kg_ref, o_ref,
                   qn_ref, kn_ref, sc_a, sc_b):
    s, hd = q_ref.shape[1:]
    w = GRID_W
    nq, nk = bias_ref.shape[2:]
    n_batches = s // (nq * NATTEN_BATCH)
    qn_ref[...] = (_rms(q_ref[0].astype(F32)) * qg_ref[...] * (float(hd) ** -0.5)).astype(BF16)
    kn_ref[...] = (_rms(k_ref[0].astype(F32)) * kg_ref[...]).astype(BF16)

    def rows_of(p):
        return (pl.ds(pl.multiple_of(p * nq, nq), nq), pl.ds(pl.multiple_of(ks_ref[p] * w, w), nk))

    def scores(p):
        q_rows, kv_rows = rows_of(p)
        qk = lax.dot_general(qn_ref[q_rows, :], kn_ref[kv_rows, :], (((1,), (1,)), ((), ())),
                             preferred_element_type=F32)
        return qk + bias_ref[0, var_ref[p]]

    def attend(p, sc):
        q_rows, kv_rows = rows_of(p)
        p_un = jnp.exp(sc - jnp.max(sc, axis=1, keepdims=True))
        o = jnp.dot(p_un.astype(BF16), v_ref[0, kv_rows, :], preferred_element_type=F32)
        o = o * (1.0 / jnp.sum(p_un, axis=1, keepdims=True))
        o_ref[0, q_rows, :] = (o * _silu(z_ref[0, q_rows, :].astype(F32))).astype(o_ref.dtype)

    def fill(buf, j):
        for u in range(NATTEN_BATCH):
            buf[u] = scores(j * NATTEN_BATCH + u)

    def drain(buf, j):
        for u in range(NATTEN_BATCH):
            attend(j * NATTEN_BATCH + u, buf[u])

    def body(k, carry):
        fill(sc_b, 2 * k + 1)
        drain(sc_a, 2 * k)
        fill(sc_a, 2 * k + 2)
        drain(sc_b, 2 * k + 1)
        return carry

    fill(sc_a, 0)
    lax.fori_loop(0, n_batches // 2 - 1, body, 0)
    fill(sc_b, n_batches - 1)
    drain(sc_a, n_batches - 2)
    drain(sc_b, n_batches - 1)


def _natten(proj, rpb, q_gain, k_gain, heads, hd, col0):
    b, s, _ = proj.shape
    win_r = (rpb.shape[1] + 1) // 2
    n_rows = s // GRID_W
    k_rows = win_r + NATTEN_Q_ROWS
    assert s % GRID_W == 0 and n_rows >= k_rows and col0 % hd == 0
    assert n_rows % (2 * NATTEN_Q_ROWS * NATTEN_BATCH) == 0
    variants, var_idx, k_start = _natten_plan(n_rows, win_r, NATTEN_Q_ROWS, k_rows)
    bias = _natten_bias(rpb, variants)
    sc_buf = pltpu.VMEM((NATTEN_BATCH,) + bias.shape[2:], F32)
    c0 = col0 // hd

    def col(part):
        return pl.BlockSpec((1, s, hd), lambda h, i: (i, 0, c0 + part * heads + h))

    smem = pl.BlockSpec(memory_space=pltpu.SMEM)
    vec = pl.BlockSpec((1, hd), lambda h, i: (0, 0))
    return pl.pallas_call(
        _natten_kernel,
        grid=(heads, b),
        in_specs=[smem, smem, col(0), col(1), col(2), col(3),
                  pl.BlockSpec((1,) + bias.shape[1:], lambda h, i: (h, 0, 0, 0)), vec, vec],
        out_specs=pl.BlockSpec((1, s, hd), lambda h, i: (i, 0, h)),
        out_shape=jax.ShapeDtypeStruct((b, s, heads * hd), BF16),
        scratch_shapes=[pltpu.VMEM((s, hd), BF16), pltpu.VMEM((s, hd), BF16), sc_buf, sc_buf],
        compiler_params=_params(2),
        name="neighbourhood_attention",
    )(jnp.asarray(var_idx, jnp.int32), jnp.asarray(k_start, jnp.int32), proj, proj, proj, proj, bias,
      q_gain.reshape(1, hd), k_gain.reshape(1, hd))


def _mm_out_kernel(a1_ref, a2_ref, w1_ref, w2_ref, x_ref, gate_ref, o_ref):
    acc = jnp.dot(a1_ref[0], w1_ref[...], preferred_element_type=F32)
    acc = acc + jnp.dot(a2_ref[0], w2_ref[...], preferred_element_type=F32)
    o_ref[0] = x_ref[0] + gate_ref[0] * acc


def _out_proj(a1, a1_blk, a2, a2_blk, w, x, gate, name):
    b, s, d = x.shape
    kh = w.shape[0] // 2
    tm, tn = _tile(s, 512), _tile(d, 512)
    return pl.pallas_call(
        _mm_out_kernel,
        grid=(b, s // tm, d // tn),
        in_specs=[pl.BlockSpec((1, tm, kh), lambda i, m, n: (i, m, a1_blk)),
                  pl.BlockSpec((1, tm, kh), lambda i, m, n: (i, m, a2_blk)),
                  pl.BlockSpec((kh, tn), lambda i, m, n: (0, n)),
                  pl.BlockSpec((kh, tn), lambda i, m, n: (1, n)),
                  pl.BlockSpec((1, tm, tn), lambda i, m, n: (i, m, n)),
                  pl.BlockSpec((1, 1, tn), lambda i, m, n: (i, 0, n))],
        out_specs=pl.BlockSpec((1, tm, tn), lambda i, m, n: (i, m, n)),
        out_shape=jax.ShapeDtypeStruct((b, s, d), F32),
        compiler_params=_params(3),
        name=name,
    )(a1, a2, w, w, x, gate.reshape(b, 1, d))


def _spatial_kernel(gv_ref, uz_ref, ws_ref, bs_ref, vg_ref, o_ref):
    groups = ws_ref.shape[0]
    cg = gv_ref.shape[2] // groups
    vn = (_rms(gv_ref[0].astype(F32)) * vg_ref[...]).astype(BF16)
    for g in range(groups):
        cols = slice(g * cg, (g + 1) * cg)
        sv = jnp.dot(ws_ref[g], vn[:, cols], preferred_element_type=F32) + bs_ref[:, g:g + 1]
        o_ref[0, :, cols] = (uz_ref[0, :, cols].astype(F32) * sv).astype(BF16)


def _spatial(gv, uz, ws, bs_t, v_gain):
    b, s, cw = gv.shape
    l = SPATIAL_CHUNK
    groups = ws.shape[0]
    assert s % l == 0 and ws.shape[1:] == (l, l)
    blk = pl.BlockSpec((1, l, cw), lambda i, n: (i, n, 0))
    return pl.pallas_call(
        _spatial_kernel,
        grid=(b, s // l),
        in_specs=[blk, blk,
                  pl.BlockSpec(ws.shape, lambda i, n: (0, 0, 0)),
                  pl.BlockSpec((l, groups), lambda i, n: (0, 0)),
                  pl.BlockSpec((1, cw), lambda i, n: (0, 0))],
        out_specs=blk,
        out_shape=jax.ShapeDtypeStruct((b, s, cw), BF16),
        compiler_params=_params(2),
        name="gmlp_spatial_gate",
    )(gv, uz, ws, bs_t, v_gain.reshape(1, cw))


def kernel(x, c, norm_g0, ada_w0, ada_b0, w_in0, a_conv_w, a_gate_b, a_norm_g, b_q_gain, b_k_gain, b_rpb, w_out0,
           norm_g1, ada_w1, ada_b1, w_in1, c_v_norm_g, c_w_s, c_b_s, w_out1):
    b, s, d = x.shape
    a_heads, dv = a_norm_g.shape
    a_width = a_heads * dv
    dk = dv // 2
    qk_width = a_heads * dk
    hd = b_q_gain.shape[0]
    b_heads, b_width = b_rpb.shape[0], b_rpb.shape[0] * hd
    n_gate = 4 * a_heads
    a_cols = 2 * qk_width + 3 * a_width
    assert w_in0.shape[1] == a_cols + n_gate + 4 * b_width
    cw = c_v_norm_g.shape[0]
    nc = s // MLSTM_CHUNK

    c_pad = jnp.pad(c, ((0, -b % 8), (0, 0)))
    mod0 = _modulation(c_pad, ada_w0, ada_b0)[:b]
    mod1 = _modulation(c_pad, ada_w1, ada_b1)[:b]
    shift0, scale0, gate0 = mod0[:, :d], mod0[:, d:2 * d], mod0[:, 2 * d:]
    shift1, scale1, gate1 = mod1[:, :d], mod1[:, d:2 * d], mod1[:, 2 * d:]

    h0 = _norm_mod(x, norm_g0, scale0, shift0).reshape(b * s, d)
    qkv_cols = 2 * qk_width + a_width
    wt0 = w_in0.T
    qkv = _proj(h0, wt0, [0], qkv_cols, _tile(qkv_cols, 1024), _identity, BF16, "layer0_in_proj_qkv",
                w_transposed=True).reshape(b, s, qkv_cols)
    og = _proj(h0, wt0, [qkv_cols, qkv_cols + a_width], a_width, _tile(a_width, 512), _og_gate, BF16,
               "layer0_in_proj_og", w_transposed=True).reshape(b, s, a_width)
    proj_b = _proj(h0, wt0, [a_cols + n_gate], 4 * b_width, _tile(4 * b_width, 1024), _identity, BF16,
                   "layer0_in_proj_b", w_transposed=True).reshape(b, s, 4 * b_width)
    assert a_cols % LANES == 0 and n_gate <= LANES
    gates = _proj(h0, wt0, [a_cols], LANES, LANES, _identity, F32, "layer0_gate_proj",
                  w_transposed=True)[:, :n_gate]

    q, k, kt = _conv_silu(qkv, a_conv_w, qk_width, dk)
    graw = gates.reshape(b, nc, MLSTM_CHUNK, n_gate).transpose(0, 1, 3, 2)
    ig, pre, suf, mst = _gate_prep(graw, a_gate_b)
    hh = a_heads
    pack = jnp.stack([ig[:, :, :hh], pre[:, :, hh:2 * hh], ig[:, :, 2 * hh:3 * hh], suf[:, :, 3 * hh:],
                      mst[:, :, :hh], mst[:, :, hh:2 * hh], mst[:, :, 2 * hh:3 * hh], mst[:, :, 3 * hh:]],
                     axis=3)
    pack = pack.transpose(0, 2, 1, 3, 4)
    y_a = _mlstm(q, k, kt, qkv, og, pack, a_norm_g, a_heads, dk, dv, 2 * qk_width)

    y_b = _natten(proj_b, b_rpb, b_q_gain, b_k_gain, b_heads, hd, 0)

    assert a_width == b_width
    x1 = _out_proj(y_a, 0, y_b, 0, w_out0.astype(BF16), x, gate0, "layer0_out_proj")

    h1 = _norm_mod(x1, norm_g1, scale1, shift1).reshape(b * s, d)
    uz = _proj(h1, w_in1, [0, 2 * cw], cw, _tile(cw, 512), _uz_gate, BF16, "gmlp_in_proj_uz")
    gv = _proj(h1, w_in1, [cw], cw, _tile(cw, 1024), jax.nn.gelu, BF16, "gmlp_in_proj_v")
    y = _spatial(gv.reshape(b, s, cw), uz.reshape(b, s, cw), c_w_s.astype(BF16), c_b_s.T, c_v_norm_g)
    return _out_proj(y, 0, y, 1, w_out1.astype(BF16), x1, gate1, "layer1_out_proj")
```

```python
import functools

import jax
import jax.numpy as jnp
from jax import lax
from jax.experimental import pallas as pl
from jax.experimental.pallas import tpu as pltpu

F32 = jnp.float32
BF16 = jnp.bfloat16
HIGHEST = lax.Precision.HIGHEST

EPS = 1e-6
GRID_W = 64
MLSTM_CHUNK = 256
SPATIAL_CHUNK = 128
LANES = 128
V7X_VMEM_LIMIT_BYTES = 52 * 1024 * 1024


def _params(n_axes, vmem=V7X_VMEM_LIMIT_BYTES):
    return pltpu.CompilerParams(dimension_semantics=("arbitrary",) * n_axes, vmem_limit_bytes=vmem)


def _tile(dim, pref):
    t = min(dim, pref)
    assert dim % t == 0, (dim, pref)
    return t


def _silu(x):
    return x * jax.nn.sigmoid(x)


def _rms(x):
    return x * lax.rsqrt(jnp.mean(x * x, axis=-1, keepdims=True) + EPS)


def _ada_kernel(c_ref, w_ref, b_ref, o_ref):
    a = _silu(c_ref[...]).astype(BF16)
    o_ref[...] = jnp.dot(a, w_ref[...].astype(BF16), preferred_element_type=F32) + b_ref[...]


def _modulation(c_pad, w, b):
    rows, d = c_pad.shape
    n = w.shape[1]
    tn = _tile(n, 512)
    return pl.pallas_call(
        _ada_kernel,
        grid=(n // tn,),
        in_specs=[pl.BlockSpec((rows, d), lambda j: (0, 0)),
                  pl.BlockSpec((d, tn), lambda j: (0, j)),
                  pl.BlockSpec((1, tn), lambda j: (0, j))],
        out_specs=pl.BlockSpec((rows, tn), lambda j: (0, j)),
        out_shape=jax.ShapeDtypeStruct((rows, n), F32),
        compiler_params=_params(1),
        name="ada_modulation",
    )(c_pad, w, b.reshape(1, n))


def _norm_mod_kernel(x_ref, g_ref, sc_ref, sh_ref, o_ref):
    y = _rms(x_ref[0]) * g_ref[...]
    o_ref[0] = (y * (1.0 + sc_ref[0]) + sh_ref[0]).astype(BF16)


def _norm_mod(x, g, scale, shift):
    b, s, d = x.shape
    ts = _tile(s, 256)
    vec = pl.BlockSpec((1, 1, d), lambda i, j: (i, 0, 0))
    return pl.pallas_call(
        _norm_mod_kernel,
        grid=(b, s // ts),
        in_specs=[pl.BlockSpec((1, ts, d), lambda i, j: (i, j, 0)),
                  pl.BlockSpec((1, d), lambda i, j: (0, 0)), vec, vec],
        out_specs=pl.BlockSpec((1, ts, d), lambda i, j: (i, j, 0)),
        out_shape=jax.ShapeDtypeStruct((b, s, d), BF16),
        compiler_params=_params(2),
        name="norm_modulate",
    )(x, g.reshape(1, d), scale.reshape(b, 1, d), shift.reshape(b, 1, d))


BF16_SUBLANES = 16


F32_SUBLANES = 8


def _dot_dims(w_transposed):
    return (((1,), (1,)), ((), ())) if w_transposed else (((1,), (0,)), ((), ()))


def _proj_head_kernel(a_ref, *refs, n_w, w_transposed, epilogue):
    w_refs, o_ref, wb_refs = refs[:n_w], refs[n_w], refs[n_w + 1:]
    a = a_ref[...]
    accs = []
    for w_ref, wb_ref in zip(w_refs, wb_refs):
        wb = w_ref[...].astype(BF16)
        wb_ref[...] = wb
        accs.append(lax.dot_general(a, wb, _dot_dims(w_transposed), preferred_element_type=F32))
    o_ref[...] = epilogue(*accs).astype(o_ref.dtype)


def _proj_tail_kernel(a_ref, *refs, n_w, w_transposed, epilogue):
    w_refs, o_ref = refs[:n_w], refs[-1]
    a = a_ref[...]
    accs = [lax.dot_general(a, w_ref[...], _dot_dims(w_transposed), preferred_element_type=F32)
            for w_ref in w_refs]
    o_ref[...] = epilogue(*accs).astype(o_ref.dtype)


def _proj(a, w, offsets, n, tn, epilogue, out_dtype, name, w_transposed=False):
    m, k = a.shape
    n_w = len(offsets)
    tm = _tile(m, 1024)
    assert n % tn == 0 and m % tm == 0
    tn_head = _tile(tn, 512 // n_w)

    def head_w_spec(off):
        if not w_transposed:
            assert off % tn_head == 0
            return pl.BlockSpec((k, tn_head), lambda j: (0, off // tn_head + j))
        sub = F32_SUBLANES
        assert off % sub == 0 and tn_head % sub == 0
        return pl.BlockSpec((pl.Element(tn_head), pl.Element(k)),
                            lambda j: ((off // sub + j * (tn_head // sub)) * sub, 0))

    if w_transposed:
        wb_shape, wb_head, wb_tail = (n, k), pl.BlockSpec((tn_head, k), lambda j: (j, 0)), \
            pl.BlockSpec((tn, k), lambda i, j: (j, 0))
    else:
        wb_shape, wb_head, wb_tail = (k, n), pl.BlockSpec((k, tn_head), lambda j: (0, j)), \
            pl.BlockSpec((k, tn), lambda i, j: (0, j))
    out_shape = jax.ShapeDtypeStruct((m, n), out_dtype)
    head = pl.pallas_call(
        functools.partial(_proj_head_kernel, n_w=n_w, w_transposed=w_transposed, epilogue=epilogue),
        grid=(n // tn_head,),
        in_specs=[pl.BlockSpec((tm, k), lambda j: (0, 0))] + [head_w_spec(off) for off in offsets],
        out_specs=[pl.BlockSpec((tm, tn_head), lambda j: (0, j))] + [wb_head] * n_w,
        out_shape=[out_shape] + [jax.ShapeDtypeStruct(wb_shape, BF16)] * n_w,
        compiler_params=_params(1),
        name=name + "_head",
    )(a, *([w] * n_w))
    out, wbs = head[0], head[1:]
    if m == tm:
        return out
    return pl.pallas_call(
        functools.partial(_proj_tail_kernel, n_w=n_w, w_transposed=w_transposed, epilogue=epilogue),
        grid=(m // tm - 1, n // tn),
        in_specs=[pl.BlockSpec((tm, k), lambda i, j: (i + 1, 0))] + [wb_tail] * n_w
        + [pl.BlockSpec(memory_space=pl.ANY)],
        out_specs=pl.BlockSpec((tm, tn), lambda i, j: (i + 1, j)),
        out_shape=out_shape,
        input_output_aliases={1 + n_w: 0},
        compiler_params=_params(2),
        name=name,
    )(a, *wbs, out)


def _identity(x):
    return x


def _og_gate(o, z):
    return jax.nn.sigmoid(o) * _silu(z)


def _uz_gate(u, z):
    return jax.nn.gelu(u) * _silu(z)


CONV_ROWS = 512
CONV_HALO = BF16_SUBLANES


def _conv3_silu_rows(x_ref, w, r0, rows):
    s = x_ref.shape[1]
    lo, hi = max(r0 - CONV_HALO, 0), min(r0 + rows + CONV_HALO, s)
    x = x_ref[0, lo:hi, :].astype(F32)
    n = hi - lo
    x_prev, x_next = pltpu.roll(x, 1, axis=0), pltpu.roll(x, n - 1, axis=0)
    row = lax.broadcasted_iota(jnp.int32, x.shape, 0)
    if lo == 0:
        x_prev = jnp.where(row == 0, 0.0, x_prev)
    if hi == s:
        x_next = jnp.where(row == n - 1, 0.0, x_next)
    y = _silu(x_prev * w[0:1] + x * w[1:2] + x_next * w[2:3])
    return y[r0 - lo:r0 - lo + rows]


def _conv_kernel(xq_ref, xk_ref, wq_ref, wk_ref, q_ref, k_ref, kt_ref, *, q_scale):
    s = xq_ref.shape[1]
    l = kt_ref.shape[3]
    rows = min(CONV_ROWS, s)
    wq, wk = wq_ref[...], wk_ref[...]
    for r0 in range(0, s, rows):
        q_ref[0, r0:r0 + rows, :] = (_conv3_silu_rows(xq_ref, wq, r0, rows) * q_scale).astype(BF16)
        k = _conv3_silu_rows(xk_ref, wk, r0, rows)
        k_ref[0, r0:r0 + rows, :] = k.astype(BF16)
        for c in range(rows // l):
            kt_ref[0, r0 // l + c] = k[c * l:(c + 1) * l, :].T.astype(BF16)


def _conv_silu(proj, conv_w, qk_width, dk):
    b, s, _ = proj.shape
    assert conv_w.shape[0] == 3, "centred depthwise conv is written for 3 taps"
    assert s % min(CONV_ROWS, s) == 0 and min(CONV_ROWS, s) % MLSTM_CHUNK == 0
    tc = _tile(qk_width, 256)
    nq = qk_width // tc
    l = MLSTM_CHUNK
    kern = functools.partial(_conv_kernel, q_scale=float(dk) ** -0.5)
    col = pl.BlockSpec((1, s, tc), lambda i, j: (i, 0, j))
    return pl.pallas_call(
        kern,
        grid=(b, nq),
        in_specs=[col, pl.BlockSpec((1, s, tc), lambda i, j: (i, 0, nq + j)),
                  pl.BlockSpec((3, tc), lambda i, j: (0, j)), pl.BlockSpec((3, tc), lambda i, j: (0, nq + j))],
        out_specs=[col, col, pl.BlockSpec((1, s // l, tc, l), lambda i, j: (i, 0, j, 0))],
        out_shape=[jax.ShapeDtypeStruct((b, s, qk_width), BF16), jax.ShapeDtypeStruct((b, s, qk_width), BF16),
                   jax.ShapeDtypeStruct((b, s // l, qk_width, l), BF16)],
        compiler_params=_params(2),
        name="qk_conv_silu",
    )(proj, proj, conv_w, conv_w)


def _gates_kernel(x_ref, b_ref, ig_ref, pre_ref, suf_ref, m_ref):
    nc, r, l = x_ref.shape[1:]
    hh = r // 4
    x = x_ref[0] + b_ref[...][None]
    ig_ref[0] = x
    lf = (jnp.minimum(x, 0.0) - jnp.log1p(jnp.exp(-jnp.abs(x)))).reshape(nc * r, l)
    u = lax.broadcasted_iota(jnp.int32, (l, l), 0)
    t = lax.broadcasted_iota(jnp.int32, (l, l), 1)
    pre = jnp.dot(lf, (u <= t).astype(F32), precision=HIGHEST, preferred_element_type=F32).reshape(nc, r, l)
    suf = jnp.dot(lf, (u >= t).astype(F32), precision=HIGHEST, preferred_element_type=F32).reshape(nc, r, l)
    pre_ref[0] = pre
    suf_ref[0] = suf
    g_f, g_b = pre[:, hh:2 * hh], suf[:, 3 * hh:]
    tot_f, tot_b = g_f[:, :, l - 1:l], g_b[:, :, 0:1]
    a_f = jnp.max(tot_f - g_f + x[:, :hh], axis=2, keepdims=True)
    a_b = jnp.max(tot_b - g_b + x[:, 2 * hh:3 * hh], axis=2, keepdims=True)
    m = jnp.zeros((hh, 1), F32)
    for c in range(nc):
        m_ref[0, c, 0:hh] = jnp.broadcast_to(m, (hh, l))
        m = jnp.maximum(tot_f[c] + m, a_f[c])
        m_ref[0, c, hh:2 * hh] = jnp.broadcast_to(m, (hh, l))
    m = jnp.zeros((hh, 1), F32)
    for c in reversed(range(nc)):
        m_ref[0, c, 2 * hh:3 * hh] = jnp.broadcast_to(m, (hh, l))
        m = jnp.maximum(tot_b[c] + m, a_b[c])
        m_ref[0, c, 3 * hh:] = jnp.broadcast_to(m, (hh, l))


def _gate_prep(graw, bias):
    b, nc, r, l = graw.shape
    spec = pl.BlockSpec((1, nc, r, l), lambda i: (i, 0, 0, 0))
    shape = jax.ShapeDtypeStruct(graw.shape, F32)
    return pl.pallas_call(
        _gates_kernel,
        grid=(b,),
        in_specs=[spec, pl.BlockSpec((r, 1), lambda i: (0, 0))],
        out_specs=[spec] * 4,
        out_shape=[shape] * 4,
        compiler_params=_params(1),
        name="mlstm_gate_prep",
    )(graw, bias.reshape(r, 1))


def _row_to_col(row_vec, eye):
    return jnp.sum(jnp.where(eye, row_vec, 0.0), axis=1, keepdims=True)


def _mlstm_local(q, kt, ig, g, tot, m, m_new, *, causal):
    l = q.shape[0]
    r_i = lax.broadcasted_iota(jnp.int32, (l, l), 0)
    c_i = lax.broadcasted_iota(jnp.int32, (l, l), 1)
    g_col = _row_to_col(g, r_i == c_i)
    d = g_col - g + ig
    d = jnp.where((c_i <= r_i) if causal else (c_i >= r_i), d, -jnp.inf)
    gm = g_col + m
    m_t = jnp.maximum(gm, jnp.max(d, axis=1, keepdims=True))
    s_ = jnp.dot(q, kt, preferred_element_type=F32) * jnp.exp(d - m_t)
    w = jnp.exp(tot - g + ig - m_new)
    return dict(
        s=s_.astype(BF16), s_sum=jnp.sum(s_, axis=1, keepdims=True),
        inter=jnp.exp(gm - m_t), floor=jnp.exp(-m_t), decay=jnp.exp(tot + m - m_new),
        wkt=(kt.astype(F32) * w).astype(BF16),
        w_rows=jnp.broadcast_to(w, (8, l)).astype(BF16))


def _mlstm_state(q, k, v, loc, c, n):
    qc = jnp.dot(q, c.astype(BF16), preferred_element_type=F32)
    num = jnp.dot(loc["s"], v, preferred_element_type=F32) + loc["inter"] * qc
    qn = jnp.sum(q.astype(F32) * n, axis=1, keepdims=True)
    den = loc["s_sum"] + loc["inter"] * qn
    h = num * (1.0 / jnp.maximum(jnp.abs(den), loc["floor"]))
    c_new = loc["decay"] * c + jnp.dot(loc["wkt"], v, preferred_element_type=F32)
    n_new = loc["decay"] * n + jnp.dot(loc["w_rows"], k, preferred_element_type=F32)[0:1]
    return h, c_new, n_new


MLSTM_UNROLL = 2


def _mlstm_kernel(q_ref, k_ref, kt_ref, v_ref, og_ref, gp_ref, ng_ref, o_ref, acc_ref, cf_ref, cb_ref):
    s, dv = v_ref.shape[1:]
    dk = q_ref.shape[2]
    l = MLSTM_CHUNK
    nc = s // l
    cf_ref[...] = jnp.zeros_like(cf_ref)
    cb_ref[...] = jnp.zeros_like(cb_ref)
    gain = ng_ref[0]

    def rows_of(c):
        return pl.ds(pl.multiple_of(c * l, l), l)

    def local(c, causal):
        gp = gp_ref[0, 0, c]
        if causal:
            ig, g, m, m_new = gp[0:1], gp[1:2], gp[4:5, 0:1], gp[5:6, 0:1]
            tot = g[:, l - 1:l]
        else:
            ig, g, m, m_new = gp[2:3], gp[3:4], gp[6:7, 0:1], gp[7:8, 0:1]
            tot = g[:, 0:1]
        return _mlstm_local(q_ref[0, rows_of(c), :], kt_ref[0, c], ig, g, tot, m, m_new, causal=causal)

    def emit(rows, h, first_touch):
        if first_touch:
            acc_ref[rows, :] = h
        else:
            y = _rms(acc_ref[rows, :] + h) * gain * og_ref[0, rows, :].astype(F32)
            o_ref[0, rows, :] = y.astype(o_ref.dtype)

    def body(first_touch, i, carry):
        chunks = [(i * MLSTM_UNROLL + u, nc - 1 - (i * MLSTM_UNROLL + u)) for u in range(MLSTM_UNROLL)]
        n_f, n_b = carry
        c_f, c_b = cf_ref[...], cb_ref[...]
        for cf, cb in chunks:
            rows = rows_of(cf)
            h, c_f, n_f = _mlstm_state(q_ref[0, rows, :], k_ref[0, rows, :], v_ref[0, rows, :],
                                       local(cf, True), c_f, n_f)
            emit(rows, h, first_touch)
            rows = rows_of(cb)
            h, c_b, n_b = _mlstm_state(q_ref[0, rows, :], k_ref[0, rows, :], v_ref[0, rows, :],
                                       local(cb, False), c_b, n_b)
            emit(rows, h, first_touch)
        cf_ref[...] = c_f
        cb_ref[...] = c_b
        return n_f, n_b

    n0 = jnp.zeros((1, dk), F32)
    n_it = nc // MLSTM_UNROLL
    carry = lax.fori_loop(0, n_it // 2, functools.partial(body, True), (n0, n0))
    lax.fori_loop(n_it // 2, n_it, functools.partial(body, False), carry)


def _mlstm(q, k, kt, qkv, og, gate_pack, norm_g, heads, dk, dv, v_col0):
    b, s, _ = qkv.shape
    assert s % (2 * MLSTM_UNROLL * MLSTM_CHUNK) == 0 and v_col0 % dv == 0
    nc = s // MLSTM_CHUNK
    seq_dk = pl.BlockSpec((1, s, dk), lambda i, h: (i, 0, h))
    seq_dv = pl.BlockSpec((1, s, dv), lambda i, h: (i, 0, h))
    return pl.pallas_call(
        _mlstm_kernel,
        grid=(b, heads),
        in_specs=[seq_dk, seq_dk,
                  pl.BlockSpec((1, nc, dk, MLSTM_CHUNK), lambda i, h: (i, 0, h, 0)),
                  pl.BlockSpec((1, s, dv), lambda i, h: (i, 0, v_col0 // dv + h)),
                  seq_dv,
                  pl.BlockSpec((1, 1, nc, 8, MLSTM_CHUNK), lambda i, h: (i, h, 0, 0, 0)),
                  pl.BlockSpec((1, 1, dv), lambda i, h: (h, 0, 0))],
        out_specs=seq_dv,
        out_shape=jax.ShapeDtypeStruct((b, s, heads * dv), BF16),
        scratch_shapes=[pltpu.VMEM((s, dv), F32),
                        pltpu.VMEM((dk, dv), F32),
                        pltpu.VMEM((dk, dv), F32)],
        compiler_params=_params(2),
        name="mlstm_bidirectional",
    )(q, k, kt, qkv, og, gate_pack, norm_g.reshape(heads, 1, dv))


def _rpb_toeplitz_kernel(r_ref, o_ref, *, win_c):
    dpad, n = r_ref.shape[1], o_ref.shape[1]
    dd = lax.broadcasted_iota(jnp.int32, (dpad, n), 0)
    cc = lax.broadcasted_iota(jnp.int32, (dpad, n), 1)
    shift = GRID_W.bit_length() - 1
    qc, kc = cc >> shift, cc & (GRID_W - 1)
    dc = jnp.clip(kc - qc + (win_c - 1), 0, 2 * win_c - 2)
    onehot = jnp.where(dd == dc, 1.0, 0.0).astype(F32)
    vals = jnp.dot(r_ref[...], onehot, precision=HIGHEST, preferred_element_type=F32)
    co = lax.broadcasted_iota(jnp.int32, (1, n), 1)
    qo, ko = co >> shift, co & (GRID_W - 1)
    cs = jnp.clip(qo - win_c // 2, 0, GRID_W - win_c)
    o_ref[...] = jnp.where((ko >= cs) & (ko < cs + win_c), vals, -jnp.inf)


def _natten_plan(n_rows, win_r, q_rows, k_rows):
    variants, var_idx, k_start = [], [], []
    for r0 in range(0, n_rows, q_rows):
        kp = min(max(r0 - win_r // 2, 0), n_rows - k_rows)
        pattern = []
        for i in range(q_rows):
            r = r0 + i
            rs = min(max(r - win_r // 2, 0), n_rows - win_r)
            assert kp <= rs and rs + win_r <= kp + k_rows
            pattern.append(tuple((kp + jj - r + win_r - 1) if rs <= kp + jj < rs + win_r else None
                                 for jj in range(k_rows)))
        pattern = tuple(pattern)
        if pattern not in variants:
            variants.append(pattern)
        var_idx.append(variants.index(pattern))
        k_start.append(kp)
    return variants, var_idx, k_start


def _bias_assemble_kernel(t_ref, o_ref, *, variants):
    w = GRID_W
    masked = jnp.full((w, w), -jnp.inf, F32)
    for v, pattern in enumerate(variants):
        for i, row in enumerate(pattern):
            tiles = [masked if dr is None else t_ref[0, dr] for dr in row]
            o_ref[0, v, i * w:(i + 1) * w, :] = jnp.concatenate(tiles, axis=1)


def _natten_bias(rpb, variants):
    hb, n_dr, n_dc = rpb.shape
    win_c = (n_dc + 1) // 2
    w = GRID_W
    dpad = 32
    assert n_dc <= dpad and (hb * n_dr) % 8 == 0
    r = jnp.pad(rpb, ((0, 0), (0, 0), (0, dpad - n_dc))).reshape(hb * n_dr, dpad)
    toeplitz = pl.pallas_call(
        functools.partial(_rpb_toeplitz_kernel, win_c=win_c),
        grid=(1,),
        in_specs=[pl.BlockSpec(r.shape, lambda i: (0, 0))],
        out_specs=pl.BlockSpec((r.shape[0], w * w), lambda i: (0, 0)),
        out_shape=jax.ShapeDtypeStruct((r.shape[0], w * w), F32),
        compiler_params=_params(1),
        name="natten_rpb_toeplitz",
    )(r).reshape(hb, n_dr, w, w)
    q_rows, k_rows = len(variants[0]), len(variants[0][0])
    out_blk = (1, len(variants), q_rows * w, k_rows * w)
    return pl.pallas_call(
        functools.partial(_bias_assemble_kernel, variants=variants),
        grid=(hb,),
        in_specs=[pl.BlockSpec((1, n_dr, w, w), lambda h: (h, 0, 0, 0))],
        out_specs=pl.BlockSpec(out_blk, lambda h: (h, 0, 0, 0)),
        out_shape=jax.ShapeDtypeStruct((hb,) + out_blk[1:], F32),
        compiler_params=_params(1),
        name="natten_bias_assemble",
    )(toeplitz)


NATTEN_Q_ROWS = 1
NATTEN_BATCH = 16


def _natten_kernel(var_ref, ks_ref, q_ref, k_ref, v_ref, z_ref, bias_ref, qg_ref, kg_ref, o_ref,
                   qn_ref, kn_ref, sc_a, sc_b):
    s, hd = q_ref.shape[1:]
    w = GRID_W
    nq, nk = bias_ref.shape[2:]
    n_batches = s // (nq * NATTEN_BATCH)
    def unit_rows(x):
        return x * lax.rsqrt(jnp.sum(x * x, axis=-1, keepdims=True) + hd * EPS)

    qn_ref[...] = (unit_rows(q_ref[0].astype(F32)) * qg_ref[...]).astype(BF16)
    kn_ref[...] = (unit_rows(k_ref[0].astype(F32)) * (kg_ref[...] * (float(hd) ** 0.5))).astype(BF16)

    def rows_of(p):
        return (pl.ds(pl.multiple_of(p * nq, nq), nq), pl.ds(pl.multiple_of(ks_ref[p] * w, w), nk))

    def scores(p):
        q_rows, kv_rows = rows_of(p)
        qk = lax.dot_general(qn_ref[q_rows, :], kn_ref[kv_rows, :], (((1,), (1,)), ((), ())),
                             preferred_element_type=F32)
        return qk + bias_ref[0, var_ref[p]]

    def attend(p, sc):
        q_rows, kv_rows = rows_of(p)
        p_un = jnp.exp(sc - jnp.max(sc, axis=1, keepdims=True))
        o = jnp.dot(p_un.astype(BF16), v_ref[0, kv_rows, :], preferred_element_type=F32)
        o = o * (1.0 / jnp.sum(p_un, axis=1, keepdims=True))
        o_ref[0, q_rows, :] = (o * _silu(z_ref[0, q_rows, :].astype(F32))).astype(o_ref.dtype)

    def fill(buf, j):
        for u in range(NATTEN_BATCH):
            buf[u] = scores(j * NATTEN_BATCH + u)

    def drain(buf, j):
        for u in range(NATTEN_BATCH):
            attend(j * NATTEN_BATCH + u, buf[u])

    def body(k, carry):
        fill(sc_b, 2 * k + 1)
        drain(sc_a, 2 * k)
        fill(sc_a, 2 * k + 2)
        drain(sc_b, 2 * k + 1)
        return carry

    fill(sc_a, 0)
    lax.fori_loop(0, n_batches // 2 - 1, body, 0)
    fill(sc_b, n_batches - 1)
    drain(sc_a, n_batches - 2)
    drain(sc_b, n_batches - 1)


def _natten(proj, rpb, q_gain, k_gain, heads, hd, col0):
    b, s, _ = proj.shape
    win_r = (rpb.shape[1] + 1) // 2
    n_rows = s // GRID_W
    k_rows = win_r + NATTEN_Q_ROWS - 1
    k_rows += k_rows % 2
    assert s % GRID_W == 0 and n_rows >= k_rows and col0 % hd == 0
    assert n_rows % (2 * NATTEN_Q_ROWS * NATTEN_BATCH) == 0
    variants, var_idx, k_start = _natten_plan(n_rows, win_r, NATTEN_Q_ROWS, k_rows)
    bias = _natten_bias(rpb, variants)
    sc_buf = pltpu.VMEM((NATTEN_BATCH,) + bias.shape[2:], F32)
    c0 = col0 // hd

    def col(part):
        return pl.BlockSpec((1, s, hd), lambda h, i: (i, 0, c0 + part * heads + h))

    smem = pl.BlockSpec(memory_space=pltpu.SMEM)
    vec = pl.BlockSpec((1, hd), lambda h, i: (0, 0))
    return pl.pallas_call(
        _natten_kernel,
        grid=(heads, b),
        in_specs=[smem, smem, col(0), col(1), col(2), col(3),
                  pl.BlockSpec((1,) + bias.shape[1:], lambda h, i: (h, 0, 0, 0)), vec, vec],
        out_specs=pl.BlockSpec((1, s, hd), lambda h, i: (i, 0, h)),
        out_shape=jax.ShapeDtypeStruct((b, s, heads * hd), BF16),
        scratch_shapes=[pltpu.VMEM((s, hd), BF16), pltpu.VMEM((s, hd), BF16), sc_buf, sc_buf],
        compiler_params=_params(2),
        name="neighbourhood_attention",
    )(jnp.asarray(var_idx, jnp.int32), jnp.asarray(k_start, jnp.int32), proj, proj, proj, proj, bias,
      q_gain.reshape(1, hd), k_gain.reshape(1, hd))


def _mm_out_kernel(a1_ref, a2_ref, w1_ref, w2_ref, x_ref, gate_ref, o_ref):
    acc = jnp.dot(a1_ref[0], w1_ref[...], preferred_element_type=F32)
    acc = acc + jnp.dot(a2_ref[0], w2_ref[...], preferred_element_type=F32)
    o_ref[0] = x_ref[0] + gate_ref[0] * acc


def _out_proj(a1, a1_blk, a2, a2_blk, w, x, gate, name):
    b, s, d = x.shape
    kh = w.shape[0] // 2
    tm, tn = _tile(s, 512), _tile(d, 512)
    return pl.pallas_call(
        _mm_out_kernel,
        grid=(b, s // tm, d // tn),
        in_specs=[pl.BlockSpec((1, tm, kh), lambda i, m, n: (i, m, a1_blk)),
                  pl.BlockSpec((1, tm, kh), lambda i, m, n: (i, m, a2_blk)),
                  pl.BlockSpec((kh, tn), lambda i, m, n: (0, n)),
                  pl.BlockSpec((kh, tn), lambda i, m, n: (1, n)),
                  pl.BlockSpec((1, tm, tn), lambda i, m, n: (i, m, n)),
                  pl.BlockSpec((1, 1, tn), lambda i, m, n: (i, 0, n))],
        out_specs=pl.BlockSpec((1, tm, tn), lambda i, m, n: (i, m, n)),
        out_shape=jax.ShapeDtypeStruct((b, s, d), F32),
        compiler_params=_params(3),
        name=name,
    )(a1, a2, w, w, x, gate.reshape(b, 1, d))


def _spatial_kernel(gv_ref, uz_ref, ws_ref, bs_ref, vg_ref, o_ref):
    groups = ws_ref.shape[0]
    cg = gv_ref.shape[2] // groups
    vn = (_rms(gv_ref[0].astype(F32)) * vg_ref[...]).astype(BF16)
    for g in range(groups):
        cols = slice(g * cg, (g + 1) * cg)
        sv = jnp.dot(ws_ref[g], vn[:, cols], preferred_element_type=F32) + bs_ref[:, g:g + 1]
        o_ref[0, :, cols] = (uz_ref[0, :, cols].astype(F32) * sv).astype(BF16)


def _spatial(gv, uz, ws, bs_t, v_gain):
    b, s, cw = gv.shape
    l = SPATIAL_CHUNK
    groups = ws.shape[0]
    assert s % l == 0 and ws.shape[1:] == (l, l)
    blk = pl.BlockSpec((1, l, cw), lambda i, n: (i, n, 0))
    return pl.pallas_call(
        _spatial_kernel,
        grid=(b, s // l),
        in_specs=[blk, blk,
                  pl.BlockSpec(ws.shape, lambda i, n: (0, 0, 0)),
                  pl.BlockSpec((l, groups), lambda i, n: (0, 0)),
                  pl.BlockSpec((1, cw), lambda i, n: (0, 0))],
        out_specs=blk,
        out_shape=jax.ShapeDtypeStruct((b, s, cw), BF16),
        compiler_params=_params(2),
        name="gmlp_spatial_gate",
    )(gv, uz, ws, bs_t, v_gain.reshape(1, cw))


def kernel(x, c, norm_g0, ada_w0, ada_b0, w_in0, a_conv_w, a_gate_b, a_norm_g, b_q_gain, b_k_gain, b_rpb, w_out0,
           norm_g1, ada_w1, ada_b1, w_in1, c_v_norm_g, c_w_s, c_b_s, w_out1):
    b, s, d = x.shape
    a_heads, dv = a_norm_g.shape
    a_width = a_heads * dv
    dk = dv // 2
    qk_width = a_heads * dk
    hd = b_q_gain.shape[0]
    b_heads, b_width = b_rpb.shape[0], b_rpb.shape[0] * hd
    n_gate = 4 * a_heads
    a_cols = 2 * qk_width + 3 * a_width
    assert w_in0.shape[1] == a_cols + n_gate + 4 * b_width
    cw = c_v_norm_g.shape[0]
    nc = s // MLSTM_CHUNK

    c_pad = jnp.pad(c, ((0, -b % 8), (0, 0)))
    mod0 = _modulation(c_pad, ada_w0, ada_b0)[:b]
    mod1 = _modulation(c_pad, ada_w1, ada_b1)[:b]
    shift0, scale0, gate0 = mod0[:, :d], mod0[:, d:2 * d], mod0[:, 2 * d:]
    shift1, scale1, gate1 = mod1[:, :d], mod1[:, d:2 * d], mod1[:, 2 * d:]

    h0 = _norm_mod(x, norm_g0, scale0, shift0).reshape(b * s, d)
    qkv_cols = 2 * qk_width + a_width
    wt0 = w_in0.T
    qkv = _proj(h0, wt0, [0], qkv_cols, _tile(qkv_cols, 1024), _identity, BF16, "layer0_in_proj_qkv",
                w_transposed=True).reshape(b, s, qkv_cols)
    og = _proj(h0, wt0, [qkv_cols, qkv_cols + a_width], a_width, _tile(a_width, 512), _og_gate, BF16,
               "layer0_in_proj_og", w_transposed=True).reshape(b, s, a_width)
    proj_b = _proj(h0, wt0, [a_cols + n_gate], 4 * b_width, _tile(4 * b_width, 1024), _identity, BF16,
                   "layer0_in_proj_b", w_transposed=True).reshape(b, s, 4 * b_width)
    assert a_cols % LANES == 0 and n_gate <= LANES
    gates = _proj(h0, wt0, [a_cols], LANES, LANES, _identity, F32, "layer0_gate_proj",
                  w_transposed=True)[:, :n_gate]

    q, k, kt = _conv_silu(qkv, a_conv_w, qk_width, dk)
    graw = gates.reshape(b, nc, MLSTM_CHUNK, n_gate).transpose(0, 1, 3, 2)
    ig, pre, suf, mst = _gate_prep(graw, a_gate_b)
    hh = a_heads
    pack = jnp.stack([ig[:, :, :hh], pre[:, :, hh:2 * hh], ig[:, :, 2 * hh:3 * hh], suf[:, :, 3 * hh:],
                      mst[:, :, :hh], mst[:, :, hh:2 * hh], mst[:, :, 2 * hh:3 * hh], mst[:, :, 3 * hh:]],
                     axis=3)
    pack = pack.transpose(0, 2, 1, 3, 4)
    y_a = _mlstm(q, k, kt, qkv, og, pack, a_norm_g, a_heads, dk, dv, 2 * qk_width)

    y_b = _natten(proj_b, b_rpb, b_q_gain, b_k_gain, b_heads, hd, 0)

    assert a_width == b_width
    x1 = _out_proj(y_a, 0, y_b, 0, w_out0.astype(BF16), x, gate0, "layer0_out_proj")

    h1 = _norm_mod(x1, norm_g1, scale1, shift1).reshape(b * s, d)
    uz = _proj(h1, w_in1, [0, 2 * cw], cw, _tile(cw, 512), _uz_gate, BF16, "gmlp_in_proj_uz")
    gv = _proj(h1, w_in1, [cw], cw, _tile(cw, 1024), jax.nn.gelu, BF16, "gmlp_in_proj_v")
    y = _spatial(gv.reshape(b, s, cw), uz.reshape(b, s, cw), c_w_s.astype(BF16), c_b_s.T, c_v_norm_g)
    return _out_proj(y, 0, y, 1, w_out1.astype(BF16), x1, gate1, "layer1_out_proj")
```

```python
import functools

import jax
import jax.numpy as jnp
from jax import lax
from jax.experimental import pallas as pl
from jax.experimental.pallas import tpu as pltpu

F32 = jnp.float32
BF16 = jnp.bfloat16
HIGHEST = lax.Precision.HIGHEST

EPS = 1e-6
GRID_W = 64
MLSTM_CHUNK = 256
SPATIAL_CHUNK = 128
LANES = 128
V7X_VMEM_LIMIT_BYTES = 52 * 1024 * 1024


def _params(n_axes, vmem=V7X_VMEM_LIMIT_BYTES):
    return pltpu.CompilerParams(dimension_semantics=("arbitrary",) * n_axes, vmem_limit_bytes=vmem)


def _tile(dim, pref):
    t = min(dim, pref)
    assert dim % t == 0, (dim, pref)
    return t


def _silu(x):
    return x * jax.nn.sigmoid(x)


def _rms(x):
    return x * lax.rsqrt(jnp.mean(x * x, axis=-1, keepdims=True) + EPS)


def _ada_kernel(c_ref, w_ref, b_ref, o_ref):
    a = _silu(c_ref[...]).astype(BF16)
    o_ref[...] = jnp.dot(a, w_ref[...].astype(BF16), preferred_element_type=F32) + b_ref[...]


def _modulation(c_pad, w, b):
    rows, d = c_pad.shape
    n = w.shape[1]
    tn = _tile(n, 512)
    return pl.pallas_call(
        _ada_kernel,
        grid=(n // tn,),
        in_specs=[pl.BlockSpec((rows, d), lambda j: (0, 0)),
                  pl.BlockSpec((d, tn), lambda j: (0, j)),
                  pl.BlockSpec((1, tn), lambda j: (0, j))],
        out_specs=pl.BlockSpec((rows, tn), lambda j: (0, j)),
        out_shape=jax.ShapeDtypeStruct((rows, n), F32),
        compiler_params=_params(1),
        name="ada_modulation",
    )(c_pad, w, b.reshape(1, n))


def _norm_mod_kernel(x_ref, g_ref, sc_ref, sh_ref, *refs):
    y = _rms(x_ref[0]) * g_ref[...]
    h = (y * (1.0 + sc_ref[0]) + sh_ref[0]).astype(BF16)
    if len(refs) == 1:
        refs[0][0] = h
        return
    wt_ref, o_ref, p_ref = refs
    o_ref[0] = h
    p_ref[0] = lax.dot_general(h, wt_ref[...].astype(BF16), _dot_dims(True), preferred_element_type=F32)


def _norm_mod(x, g, scale, shift, wt=None, wt_row0=0):
    b, s, d = x.shape
    ts = _tile(s, 256)
    vec = pl.BlockSpec((1, 1, d), lambda i, j: (i, 0, 0))
    row = pl.BlockSpec((1, ts, d), lambda i, j: (i, j, 0))
    in_specs = [row, pl.BlockSpec((1, d), lambda i, j: (0, 0)), vec, vec]
    out_specs, out_shape = row, jax.ShapeDtypeStruct((b, s, d), BF16)
    args = [x, g.reshape(1, d), scale.reshape(b, 1, d), shift.reshape(b, 1, d)]
    if wt is not None:
        assert wt_row0 % LANES == 0
        in_specs.append(pl.BlockSpec((LANES, d), lambda i, j: (wt_row0 // LANES, 0)))
        out_specs = [row, pl.BlockSpec((1, ts, LANES), lambda i, j: (i, j, 0))]
        out_shape = [out_shape, jax.ShapeDtypeStruct((b, s, LANES), F32)]
        args.append(wt)
    return pl.pallas_call(
        _norm_mod_kernel,
        grid=(b, s // ts),
        in_specs=in_specs,
        out_specs=out_specs,
        out_shape=out_shape,
        compiler_params=_params(2),
        name="norm_modulate",
    )(*args)


BF16_SUBLANES = 16


F32_SUBLANES = 8


def _dot_dims(w_transposed):
    return (((1,), (1,)), ((), ())) if w_transposed else (((1,), (0,)), ((), ()))


def _proj_head_kernel(a_ref, *refs, n_w, w_transposed, epilogue):
    w_refs, o_ref, wb_refs = refs[:n_w], refs[n_w], refs[n_w + 1:]
    a = a_ref[...]
    accs = []
    for w_ref, wb_ref in zip(w_refs, wb_refs):
        wb = w_ref[...].astype(BF16)
        wb_ref[...] = wb
        accs.append(lax.dot_general(a, wb, _dot_dims(w_transposed), preferred_element_type=F32))
    o_ref[...] = epilogue(*accs).astype(o_ref.dtype)


def _proj_tail_kernel(a_ref, *refs, n_w, w_transposed, epilogue):
    w_refs, o_ref = refs[:n_w], refs[-1]
    a = a_ref[...]
    accs = [lax.dot_general(a, w_ref[...], _dot_dims(w_transposed), preferred_element_type=F32)
            for w_ref in w_refs]
    o_ref[...] = epilogue(*accs).astype(o_ref.dtype)


def _proj(a, w, offsets, n, tn, epilogue, out_dtype, name, w_transposed=False):
    m, k = a.shape
    n_w = len(offsets)
    tm = _tile(m, 1024)
    assert n % tn == 0 and m % tm == 0
    tn_head = _tile(tn, 512 // n_w)

    def head_w_spec(off):
        if not w_transposed:
            assert off % tn_head == 0
            return pl.BlockSpec((k, tn_head), lambda j: (0, off // tn_head + j))
        sub = F32_SUBLANES
        assert off % sub == 0 and tn_head % sub == 0
        return pl.BlockSpec((pl.Element(tn_head), pl.Element(k)),
                            lambda j: ((off // sub + j * (tn_head // sub)) * sub, 0))

    if w_transposed:
        wb_shape, wb_head, wb_tail = (n, k), pl.BlockSpec((tn_head, k), lambda j: (j, 0)), \
            pl.BlockSpec((tn, k), lambda i, j: (j, 0))
    else:
        wb_shape, wb_head, wb_tail = (k, n), pl.BlockSpec((k, tn_head), lambda j: (0, j)), \
            pl.BlockSpec((k, tn), lambda i, j: (0, j))
    out_shape = jax.ShapeDtypeStruct((m, n), out_dtype)
    head = pl.pallas_call(
        functools.partial(_proj_head_kernel, n_w=n_w, w_transposed=w_transposed, epilogue=epilogue),
        grid=(n // tn_head,),
        in_specs=[pl.BlockSpec((tm, k), lambda j: (0, 0))] + [head_w_spec(off) for off in offsets],
        out_specs=[pl.BlockSpec((tm, tn_head), lambda j: (0, j))] + [wb_head] * n_w,
        out_shape=[out_shape] + [jax.ShapeDtypeStruct(wb_shape, BF16)] * n_w,
        compiler_params=_params(1),
        name=name + "_head",
    )(a, *([w] * n_w))
    out, wbs = head[0], head[1:]
    if m == tm:
        return out
    return pl.pallas_call(
        functools.partial(_proj_tail_kernel, n_w=n_w, w_transposed=w_transposed, epilogue=epilogue),
        grid=(m // tm - 1, n // tn),
        in_specs=[pl.BlockSpec((tm, k), lambda i, j: (i + 1, 0))] + [wb_tail] * n_w
        + [pl.BlockSpec(memory_space=pl.ANY)],
        out_specs=pl.BlockSpec((tm, tn), lambda i, j: (i + 1, j)),
        out_shape=out_shape,
        input_output_aliases={1 + n_w: 0},
        compiler_params=_params(2),
        name=name,
    )(a, *wbs, out)


def _identity(x):
    return x


def _og_gate(o, z):
    return jax.nn.sigmoid(o) * _silu(z)


def _uz_gate(u, z):
    return jax.nn.gelu(u) * _silu(z)


CONV_ROWS = 512
CONV_HALO = BF16_SUBLANES


def _conv3_silu_rows(x_ref, w, r0, rows):
    s = x_ref.shape[1]
    lo, hi = max(r0 - CONV_HALO, 0), min(r0 + rows + CONV_HALO, s)
    x = x_ref[0, lo:hi, :].astype(F32)
    n = hi - lo
    x_prev, x_next = pltpu.roll(x, 1, axis=0), pltpu.roll(x, n - 1, axis=0)
    row = lax.broadcasted_iota(jnp.int32, x.shape, 0)
    if lo == 0:
        x_prev = jnp.where(row == 0, 0.0, x_prev)
    if hi == s:
        x_next = jnp.where(row == n - 1, 0.0, x_next)
    y = _silu(x_prev * w[0:1] + x * w[1:2] + x_next * w[2:3])
    return y[r0 - lo:r0 - lo + rows]


def _conv_kernel(xq_ref, xk_ref, wq_ref, wk_ref, q_ref, k_ref, kt_ref, *, q_scale):
    s = xq_ref.shape[1]
    l = kt_ref.shape[3]
    rows = min(CONV_ROWS, s)
    wq, wk = wq_ref[...], wk_ref[...]
    for r0 in range(0, s, rows):
        q_ref[0, r0:r0 + rows, :] = (_conv3_silu_rows(xq_ref, wq, r0, rows) * q_scale).astype(BF16)
        k = _conv3_silu_rows(xk_ref, wk, r0, rows)
        k_ref[0, r0:r0 + rows, :] = k.astype(BF16)
        for c in range(rows // l):
            kt_ref[0, r0 // l + c] = k[c * l:(c + 1) * l, :].T.astype(BF16)


def _conv_silu(proj, conv_w, qk_width, dk):
    b, s, _ = proj.shape
    assert conv_w.shape[0] == 3, "centred depthwise conv is written for 3 taps"
    assert s % min(CONV_ROWS, s) == 0 and min(CONV_ROWS, s) % MLSTM_CHUNK == 0
    tc = _tile(qk_width, 256)
    nq = qk_width // tc
    l = MLSTM_CHUNK
    kern = functools.partial(_conv_kernel, q_scale=float(dk) ** -0.5)
    col = pl.BlockSpec((1, s, tc), lambda i, j: (i, 0, j))
    return pl.pallas_call(
        kern,
        grid=(b, nq),
        in_specs=[col, pl.BlockSpec((1, s, tc), lambda i, j: (i, 0, nq + j)),
                  pl.BlockSpec((3, tc), lambda i, j: (0, j)), pl.BlockSpec((3, tc), lambda i, j: (0, nq + j))],
        out_specs=[col, col, pl.BlockSpec((1, s // l, tc, l), lambda i, j: (i, 0, j, 0))],
        out_shape=[jax.ShapeDtypeStruct((b, s, qk_width), BF16), jax.ShapeDtypeStruct((b, s, qk_width), BF16),
                   jax.ShapeDtypeStruct((b, s // l, qk_width, l), BF16)],
        compiler_params=_params(2),
        name="qk_conv_silu",
    )(proj, proj, conv_w, conv_w)


def _gates_kernel(x_ref, b_ref, ig_ref, pre_ref, suf_ref, m_ref):
    nc, r, l = x_ref.shape[1:]
    hh = r // 4
    x = x_ref[0] + b_ref[...][None]
    ig_ref[0] = x
    lf = (jnp.minimum(x, 0.0) - jnp.log1p(jnp.exp(-jnp.abs(x)))).reshape(nc * r, l)
    u = lax.broadcasted_iota(jnp.int32, (l, l), 0)
    t = lax.broadcasted_iota(jnp.int32, (l, l), 1)
    pre = jnp.dot(lf, (u <= t).astype(F32), precision=HIGHEST, preferred_element_type=F32).reshape(nc, r, l)
    suf = jnp.dot(lf, (u >= t).astype(F32), precision=HIGHEST, preferred_element_type=F32).reshape(nc, r, l)
    pre_ref[0] = pre
    suf_ref[0] = suf
    g_f, g_b = pre[:, hh:2 * hh], suf[:, 3 * hh:]
    tot_f, tot_b = g_f[:, :, l - 1:l], g_b[:, :, 0:1]
    a_f = jnp.max(tot_f - g_f + x[:, :hh], axis=2, keepdims=True)
    a_b = jnp.max(tot_b - g_b + x[:, 2 * hh:3 * hh], axis=2, keepdims=True)
    m = jnp.zeros((hh, 1), F32)
    for c in range(nc):
        m_ref[0, c, 0:hh] = jnp.broadcast_to(m, (hh, l))
        m = jnp.maximum(tot_f[c] + m, a_f[c])
        m_ref[0, c, hh:2 * hh] = jnp.broadcast_to(m, (hh, l))
    m = jnp.zeros((hh, 1), F32)
    for c in reversed(range(nc)):
        m_ref[0, c, 2 * hh:3 * hh] = jnp.broadcast_to(m, (hh, l))
        m = jnp.maximum(tot_b[c] + m, a_b[c])
        m_ref[0, c, 3 * hh:] = jnp.broadcast_to(m, (hh, l))


def _gate_prep(graw, bias):
    b, nc, r, l = graw.shape
    spec = pl.BlockSpec((1, nc, r, l), lambda i: (i, 0, 0, 0))
    shape = jax.ShapeDtypeStruct(graw.shape, F32)
    return pl.pallas_call(
        _gates_kernel,
        grid=(b,),
        in_specs=[spec, pl.BlockSpec((r, 1), lambda i: (0, 0))],
        out_specs=[spec] * 4,
        out_shape=[shape] * 4,
        compiler_params=_params(1),
        name="mlstm_gate_prep",
    )(graw, bias.reshape(r, 1))


def _row_to_col(row_vec, eye):
    return jnp.sum(jnp.where(eye, row_vec, 0.0), axis=1, keepdims=True)


def _mlstm_local(q, kt, ig, g, tot, m, m_new, *, causal):
    l = q.shape[0]
    r_i = lax.broadcasted_iota(jnp.int32, (l, l), 0)
    c_i = lax.broadcasted_iota(jnp.int32, (l, l), 1)
    g_col = _row_to_col(g, r_i == c_i)
    d = g_col - g + ig
    d = jnp.where((c_i <= r_i) if causal else (c_i >= r_i), d, -jnp.inf)
    gm = g_col + m
    m_t = jnp.maximum(gm, jnp.max(d, axis=1, keepdims=True))
    s_ = jnp.dot(q, kt, preferred_element_type=F32) * jnp.exp(d - m_t)
    w = jnp.exp(tot - g + ig - m_new)
    return dict(
        s=s_.astype(BF16), s_sum=jnp.sum(s_, axis=1, keepdims=True),
        inter=jnp.exp(gm - m_t), floor=jnp.exp(-m_t), decay=jnp.exp(tot + m - m_new),
        wkt=(kt.astype(F32) * w).astype(BF16),
        w_rows=jnp.broadcast_to(w, (8, l)).astype(BF16))


def _mlstm_state(q, k, v, loc, c, n):
    qc = jnp.dot(q, c.astype(BF16), preferred_element_type=F32)
    num = jnp.dot(loc["s"], v, preferred_element_type=F32) + loc["inter"] * qc
    qn = jnp.sum(q.astype(F32) * n, axis=1, keepdims=True)
    den = loc["s_sum"] + loc["inter"] * qn
    h = num * (1.0 / jnp.maximum(jnp.abs(den), loc["floor"]))
    c_new = loc["decay"] * c + jnp.dot(loc["wkt"], v, preferred_element_type=F32)
    n_new = loc["decay"] * n + jnp.dot(loc["w_rows"], k, preferred_element_type=F32)[0:1]
    return h, c_new, n_new


MLSTM_UNROLL = 2


def _mlstm_kernel(q_ref, k_ref, kt_ref, v_ref, og_ref, gp_ref, ng_ref, o_ref, acc_ref, cf_ref, cb_ref):
    s, dv = v_ref.shape[1:]
    dk = q_ref.shape[2]
    l = MLSTM_CHUNK
    nc = s // l
    cf_ref[...] = jnp.zeros_like(cf_ref)
    cb_ref[...] = jnp.zeros_like(cb_ref)
    gain = ng_ref[0]

    def rows_of(c):
        return pl.ds(pl.multiple_of(c * l, l), l)

    def local(c, causal):
        gp = gp_ref[0, 0, c]
        if causal:
            ig, g, m, m_new = gp[0:1], gp[1:2], gp[4:5, 0:1], gp[5:6, 0:1]
            tot = g[:, l - 1:l]
        else:
            ig, g, m, m_new = gp[2:3], gp[3:4], gp[6:7, 0:1], gp[7:8, 0:1]
            tot = g[:, 0:1]
        return _mlstm_local(q_ref[0, rows_of(c), :], kt_ref[0, c], ig, g, tot, m, m_new, causal=causal)

    def emit(rows, h, first_touch):
        if first_touch:
            acc_ref[rows, :] = h
        else:
            y = _rms(acc_ref[rows, :] + h) * gain * og_ref[0, rows, :].astype(F32)
            o_ref[0, rows, :] = y.astype(o_ref.dtype)

    def body(first_touch, i, carry):
        chunks = [(i * MLSTM_UNROLL + u, nc - 1 - (i * MLSTM_UNROLL + u)) for u in range(MLSTM_UNROLL)]
        n_f, n_b = carry
        c_f, c_b = cf_ref[...], cb_ref[...]
        for cf, cb in chunks:
            rows = rows_of(cf)
            h, c_f, n_f = _mlstm_state(q_ref[0, rows, :], k_ref[0, rows, :], v_ref[0, rows, :],
                                       local(cf, True), c_f, n_f)
            emit(rows, h, first_touch)
            rows = rows_of(cb)
            h, c_b, n_b = _mlstm_state(q_ref[0, rows, :], k_ref[0, rows, :], v_ref[0, rows, :],
                                       local(cb, False), c_b, n_b)
            emit(rows, h, first_touch)
        cf_ref[...] = c_f
        cb_ref[...] = c_b
        return n_f, n_b

    n0 = jnp.zeros((1, dk), F32)
    n_it = nc // MLSTM_UNROLL
    carry = lax.fori_loop(0, n_it // 2, functools.partial(body, True), (n0, n0))
    lax.fori_loop(n_it // 2, n_it, functools.partial(body, False), carry)


def _mlstm(q, k, kt, qkv, og, gate_pack, norm_g, heads, dk, dv, v_col0):
    b, s, _ = qkv.shape
    assert s % (2 * MLSTM_UNROLL * MLSTM_CHUNK) == 0 and v_col0 % dv == 0
    nc = s // MLSTM_CHUNK
    seq_dk = pl.BlockSpec((1, s, dk), lambda i, h: (i, 0, h))
    seq_dv = pl.BlockSpec((1, s, dv), lambda i, h: (i, 0, h))
    return pl.pallas_call(
        _mlstm_kernel,
        grid=(b, heads),
        in_specs=[seq_dk, seq_dk,
                  pl.BlockSpec((1, nc, dk, MLSTM_CHUNK), lambda i, h: (i, 0, h, 0)),
                  pl.BlockSpec((1, s, dv), lambda i, h: (i, 0, v_col0 // dv + h)),
                  seq_dv,
                  pl.BlockSpec((1, 1, nc, 8, MLSTM_CHUNK), lambda i, h: (i, h, 0, 0, 0)),
                  pl.BlockSpec((1, 1, dv), lambda i, h: (h, 0, 0))],
        out_specs=seq_dv,
        out_shape=jax.ShapeDtypeStruct((b, s, heads * dv), BF16),
        scratch_shapes=[pltpu.VMEM((s, dv), F32),
                        pltpu.VMEM((dk, dv), F32),
                        pltpu.VMEM((dk, dv), F32)],
        compiler_params=_params(2),
        name="mlstm_bidirectional",
    )(q, k, kt, qkv, og, gate_pack, norm_g.reshape(heads, 1, dv))


def _rpb_toeplitz_kernel(r_ref, o_ref, *, win_c):
    dpad, n = r_ref.shape[1], o_ref.shape[1]
    dd = lax.broadcasted_iota(jnp.int32, (dpad, n), 0)
    cc = lax.broadcasted_iota(jnp.int32, (dpad, n), 1)
    shift = GRID_W.bit_length() - 1
    qc, kc = cc >> shift, cc & (GRID_W - 1)
    dc = jnp.clip(kc - qc + (win_c - 1), 0, 2 * win_c - 2)
    onehot = jnp.where(dd == dc, 1.0, 0.0).astype(F32)
    vals = jnp.dot(r_ref[...], onehot, precision=HIGHEST, preferred_element_type=F32)
    co = lax.broadcasted_iota(jnp.int32, (1, n), 1)
    qo, ko = co >> shift, co & (GRID_W - 1)
    cs = jnp.clip(qo - win_c // 2, 0, GRID_W - win_c)
    o_ref[...] = jnp.where((ko >= cs) & (ko < cs + win_c), vals, -jnp.inf)


def _natten_plan(n_rows, win_r, q_rows, k_rows):
    variants, var_idx, k_start = [], [], []
    for r0 in range(0, n_rows, q_rows):
        kp = min(max(r0 - win_r // 2, 0), n_rows - k_rows)
        pattern = []
        for i in range(q_rows):
            r = r0 + i
            rs = min(max(r - win_r // 2, 0), n_rows - win_r)
            assert kp <= rs and rs + win_r <= kp + k_rows
            pattern.append(tuple((kp + jj - r + win_r - 1) if rs <= kp + jj < rs + win_r else None
                                 for jj in range(k_rows)))
        pattern = tuple(pattern)
        if pattern not in variants:
            variants.append(pattern)
        var_idx.append(variants.index(pattern))
        k_start.append(kp)
    return variants, var_idx, k_start


def _bias_assemble_kernel(t_ref, o_ref, *, variants):
    w = GRID_W
    masked = jnp.full((w, w), -jnp.inf, F32)
    for v, pattern in enumerate(variants):
        for i, row in enumerate(pattern):
            tiles = [masked if dr is None else t_ref[0, dr] for dr in row]
            o_ref[0, v, i * w:(i + 1) * w, :] = jnp.concatenate(tiles, axis=1)


def _natten_bias(rpb, variants):
    hb, n_dr, n_dc = rpb.shape
    win_c = (n_dc + 1) // 2
    w = GRID_W
    dpad = 32
    assert n_dc <= dpad and (hb * n_dr) % 8 == 0
    r = jnp.pad(rpb, ((0, 0), (0, 0), (0, dpad - n_dc))).reshape(hb * n_dr, dpad)
    toeplitz = pl.pallas_call(
        functools.partial(_rpb_toeplitz_kernel, win_c=win_c),
        grid=(1,),
        in_specs=[pl.BlockSpec(r.shape, lambda i: (0, 0))],
        out_specs=pl.BlockSpec((r.shape[0], w * w), lambda i: (0, 0)),
        out_shape=jax.ShapeDtypeStruct((r.shape[0], w * w), F32),
        compiler_params=_params(1),
        name="natten_rpb_toeplitz",
    )(r).reshape(hb, n_dr, w, w)
    q_rows, k_rows = len(variants[0]), len(variants[0][0])
    out_blk = (1, len(variants), q_rows * w, k_rows * w)
    return pl.pallas_call(
        functools.partial(_bias_assemble_kernel, variants=variants),
        grid=(hb,),
        in_specs=[pl.BlockSpec((1, n_dr, w, w), lambda h: (h, 0, 0, 0))],
        out_specs=pl.BlockSpec(out_blk, lambda h: (h, 0, 0, 0)),
        out_shape=jax.ShapeDtypeStruct((hb,) + out_blk[1:], F32),
        compiler_params=_params(1),
        name="natten_bias_assemble",
    )(toeplitz)


NATTEN_Q_ROWS = 1
NATTEN_BATCH = 16


def _natten_kernel(var_ref, ks_ref, q_ref, k_ref, v_ref, z_ref, bias_ref, qg_ref, kg_ref, o_ref,
                   qn_ref, kn_ref, sc_a, sc_b):
    s, hd = q_ref.shape[1:]
    w = GRID_W
    nq, nk = bias_ref.shape[2:]
    n_batches = s // (nq * NATTEN_BATCH)
    def unit_rows(x):
        return x * lax.rsqrt(jnp.sum(x * x, axis=-1, keepdims=True) + hd * EPS)

    qn_ref[...] = (unit_rows(q_ref[0].astype(F32)) * qg_ref[...]).astype(BF16)
    kn_ref[...] = (unit_rows(k_ref[0].astype(F32)) * (kg_ref[...] * (float(hd) ** 0.5))).astype(BF16)

    def rows_of(p):
        return (pl.ds(pl.multiple_of(p * nq, nq), nq), pl.ds(pl.multiple_of(ks_ref[p] * w, w), nk))

    def scores(p):
        q_rows, kv_rows = rows_of(p)
        qk = lax.dot_general(qn_ref[q_rows, :], kn_ref[kv_rows, :], (((1,), (1,)), ((), ())),
                             preferred_element_type=F32)
        return qk + bias_ref[0, var_ref[p]]

    def attend(p, sc):
        q_rows, kv_rows = rows_of(p)
        p_un = jnp.exp(sc - jnp.max(sc, axis=1, keepdims=True))
        o = jnp.dot(p_un.astype(BF16), v_ref[0, kv_rows, :], preferred_element_type=F32)
        o = o * (1.0 / jnp.sum(p_un, axis=1, keepdims=True))
        o_ref[0, q_rows, :] = (o * _silu(z_ref[0, q_rows, :].astype(F32))).astype(o_ref.dtype)

    def fill(buf, j):
        for u in range(NATTEN_BATCH):
            buf[u] = scores(j * NATTEN_BATCH + u)

    def drain(buf, j):
        for u in range(NATTEN_BATCH):
            attend(j * NATTEN_BATCH + u, buf[u])

    def body(k, carry):
        fill(sc_b, 2 * k + 1)
        drain(sc_a, 2 * k)
        fill(sc_a, 2 * k + 2)
        drain(sc_b, 2 * k + 1)
        return carry

    fill(sc_a, 0)
    lax.fori_loop(0, n_batches // 2 - 1, body, 0)
    fill(sc_b, n_batches - 1)
    drain(sc_a, n_batches - 2)
    drain(sc_b, n_batches - 1)


def _natten(proj, rpb, q_gain, k_gain, heads, hd, col0):
    b, s, _ = proj.shape
    win_r = (rpb.shape[1] + 1) // 2
    n_rows = s // GRID_W
    k_rows = win_r + NATTEN_Q_ROWS - 1
    k_rows += k_rows % 2
    assert s % GRID_W == 0 and n_rows >= k_rows and col0 % hd == 0
    assert n_rows % (2 * NATTEN_Q_ROWS * NATTEN_BATCH) == 0
    variants, var_idx, k_start = _natten_plan(n_rows, win_r, NATTEN_Q_ROWS, k_rows)
    bias = _natten_bias(rpb, variants)
    sc_buf = pltpu.VMEM((NATTEN_BATCH,) + bias.shape[2:], F32)
    c0 = col0 // hd

    def col(part):
        return pl.BlockSpec((1, s, hd), lambda h, i: (i, 0, c0 + part * heads + h))

    smem = pl.BlockSpec(memory_space=pltpu.SMEM)
    vec = pl.BlockSpec((1, hd), lambda h, i: (0, 0))
    return pl.pallas_call(
        _natten_kernel,
        grid=(heads, b),
        in_specs=[smem, smem, col(0), col(1), col(2), col(3),
                  pl.BlockSpec((1,) + bias.shape[1:], lambda h, i: (h, 0, 0, 0)), vec, vec],
        out_specs=pl.BlockSpec((1, s, hd), lambda h, i: (i, 0, h)),
        out_shape=jax.ShapeDtypeStruct((b, s, heads * hd), BF16),
        scratch_shapes=[pltpu.VMEM((s, hd), BF16), pltpu.VMEM((s, hd), BF16), sc_buf, sc_buf],
        compiler_params=_params(2),
        name="neighbourhood_attention",
    )(jnp.asarray(var_idx, jnp.int32), jnp.asarray(k_start, jnp.int32), proj, proj, proj, proj, bias,
      q_gain.reshape(1, hd), k_gain.reshape(1, hd))


def _mm_out_head_kernel(a1_ref, a2_ref, w1_ref, w2_ref, x_ref, gate_ref, o_ref, wb1_ref, wb2_ref):
    wb1, wb2 = w1_ref[...].astype(BF16), w2_ref[...].astype(BF16)
    wb1_ref[...] = wb1
    wb2_ref[...] = wb2
    acc = jnp.dot(a1_ref[0], wb1, preferred_element_type=F32)
    acc = acc + jnp.dot(a2_ref[0], wb2, preferred_element_type=F32)
    o_ref[0] = x_ref[0] + gate_ref[0] * acc


def _mm_out_kernel(a1_ref, a2_ref, w1_ref, w2_ref, x_ref, gate_ref, _, o_ref):
    acc = jnp.dot(a1_ref[0], w1_ref[...], preferred_element_type=F32)
    acc = acc + jnp.dot(a2_ref[0], w2_ref[...], preferred_element_type=F32)
    o_ref[0] = x_ref[0] + gate_ref[0] * acc


def _out_proj(a1, a1_blk, a2, a2_blk, w, x, gate, name):
    b, s, d = x.shape
    kh = w.shape[0] // 2
    tm, tn = _tile(s, 512), _tile(d, 512)
    tn_head = _tile(tn, 256)
    mt = s // tm
    gate3 = gate.reshape(b, 1, d)
    out_shape = jax.ShapeDtypeStruct((b, s, d), F32)
    wb_shape = jax.ShapeDtypeStruct((kh, d), BF16)
    out, wb1, wb2 = pl.pallas_call(
        _mm_out_head_kernel,
        grid=(d // tn_head,),
        in_specs=[pl.BlockSpec((1, tm, kh), lambda n: (0, 0, a1_blk)),
                  pl.BlockSpec((1, tm, kh), lambda n: (0, 0, a2_blk)),
                  pl.BlockSpec((kh, tn_head), lambda n: (0, n)),
                  pl.BlockSpec((kh, tn_head), lambda n: (1, n)),
                  pl.BlockSpec((1, tm, tn_head), lambda n: (0, 0, n)),
                  pl.BlockSpec((1, 1, tn_head), lambda n: (0, 0, n))],
        out_specs=[pl.BlockSpec((1, tm, tn_head), lambda n: (0, 0, n)),
                   pl.BlockSpec((kh, tn_head), lambda n: (0, n)),
                   pl.BlockSpec((kh, tn_head), lambda n: (0, n))],
        out_shape=[out_shape, wb_shape, wb_shape],
        compiler_params=_params(1),
        name=name + "_head",
    )(a1, a2, w, w, x, gate3)
    if b * mt == 1:
        return out

    def bm(t):
        return (t + 1) // mt, (t + 1) % mt

    return pl.pallas_call(
        _mm_out_kernel,
        grid=(b * mt - 1, d // tn),
        in_specs=[pl.BlockSpec((1, tm, kh), lambda t, n: (*bm(t), a1_blk)),
                  pl.BlockSpec((1, tm, kh), lambda t, n: (*bm(t), a2_blk)),
                  pl.BlockSpec((kh, tn), lambda t, n: (0, n)),
                  pl.BlockSpec((kh, tn), lambda t, n: (0, n)),
                  pl.BlockSpec((1, tm, tn), lambda t, n: (*bm(t), n)),
                  pl.BlockSpec((1, 1, tn), lambda t, n: (bm(t)[0], 0, n)),
                  pl.BlockSpec(memory_space=pl.ANY)],
        out_specs=pl.BlockSpec((1, tm, tn), lambda t, n: (*bm(t), n)),
        out_shape=out_shape,
        input_output_aliases={6: 0},
        compiler_params=_params(2),
        name=name,
    )(a1, a2, wb1, wb2, x, gate3, out)


def _spatial_kernel(gv_ref, uz_ref, ws_ref, bs_ref, vg_ref, o_ref):
    groups, l = ws_ref.shape[:2]
    cg = gv_ref.shape[2] // groups
    for r0 in range(0, gv_ref.shape[1], l):
        rows = slice(r0, r0 + l)
        vn = (_rms(gv_ref[0, rows, :].astype(F32)) * vg_ref[...]).astype(BF16)
        for g in range(groups):
            cols = slice(g * cg, (g + 1) * cg)
            sv = jnp.dot(ws_ref[g], vn[:, cols], preferred_element_type=F32) + bs_ref[:, g:g + 1]
            o_ref[0, rows, cols] = (uz_ref[0, rows, cols].astype(F32) * sv).astype(BF16)


SPATIAL_CHUNKS_PER_STEP = 2


def _spatial(gv, uz, ws, bs_t, v_gain):
    b, s, cw = gv.shape
    l = SPATIAL_CHUNK
    groups = ws.shape[0]
    assert s % l == 0 and ws.shape[1:] == (l, l)
    rows = _tile(s, l * SPATIAL_CHUNKS_PER_STEP)
    blk = pl.BlockSpec((1, rows, cw), lambda i, n: (i, n, 0))
    return pl.pallas_call(
        _spatial_kernel,
        grid=(b, s // rows),
        in_specs=[blk, blk,
                  pl.BlockSpec(ws.shape, lambda i, n: (0, 0, 0)),
                  pl.BlockSpec((l, groups), lambda i, n: (0, 0)),
                  pl.BlockSpec((1, cw), lambda i, n: (0, 0))],
        out_specs=blk,
        out_shape=jax.ShapeDtypeStruct((b, s, cw), BF16),
        compiler_params=_params(2),
        name="gmlp_spatial_gate",
    )(gv, uz, ws, bs_t, v_gain.reshape(1, cw))


def kernel(x, c, norm_g0, ada_w0, ada_b0, w_in0, a_conv_w, a_gate_b, a_norm_g, b_q_gain, b_k_gain, b_rpb, w_out0,
           norm_g1, ada_w1, ada_b1, w_in1, c_v_norm_g, c_w_s, c_b_s, w_out1):
    b, s, d = x.shape
    a_heads, dv = a_norm_g.shape
    a_width = a_heads * dv
    dk = dv // 2
    qk_width = a_heads * dk
    hd = b_q_gain.shape[0]
    b_heads, b_width = b_rpb.shape[0], b_rpb.shape[0] * hd
    n_gate = 4 * a_heads
    a_cols = 2 * qk_width + 3 * a_width
    assert w_in0.shape[1] == a_cols + n_gate + 4 * b_width
    cw = c_v_norm_g.shape[0]
    nc = s // MLSTM_CHUNK

    c_pad = jnp.pad(c, ((0, -b % 8), (0, 0)))
    mod0 = _modulation(c_pad, ada_w0, ada_b0)[:b]
    mod1 = _modulation(c_pad, ada_w1, ada_b1)[:b]
    shift0, scale0, gate0 = mod0[:, :d], mod0[:, d:2 * d], mod0[:, 2 * d:]
    shift1, scale1, gate1 = mod1[:, :d], mod1[:, d:2 * d], mod1[:, 2 * d:]

    qkv_cols = 2 * qk_width + a_width
    wt0 = w_in0.T
    assert a_cols % LANES == 0 and n_gate <= LANES
    h0, gates = _norm_mod(x, norm_g0, scale0, shift0, wt0, a_cols)
    h0, gates = h0.reshape(b * s, d), gates.reshape(b * s, LANES)[:, :n_gate]
    qkv = _proj(h0, wt0, [0], qkv_cols, _tile(qkv_cols, 1024), _identity, BF16, "layer0_in_proj_qkv",
                w_transposed=True).reshape(b, s, qkv_cols)
    og = _proj(h0, wt0, [qkv_cols, qkv_cols + a_width], a_width, _tile(a_width, 512), _og_gate, BF16,
               "layer0_in_proj_og", w_transposed=True).reshape(b, s, a_width)
    proj_b = _proj(h0, wt0, [a_cols + n_gate], 4 * b_width, _tile(4 * b_width, 1024), _identity, BF16,
                   "layer0_in_proj_b", w_transposed=True).reshape(b, s, 4 * b_width)

    q, k, kt = _conv_silu(qkv, a_conv_w, qk_width, dk)
    graw = gates.reshape(b, nc, MLSTM_CHUNK, n_gate).transpose(0, 1, 3, 2)
    ig, pre, suf, mst = _gate_prep(graw, a_gate_b)
    hh = a_heads
    pack = jnp.stack([ig[:, :, :hh], pre[:, :, hh:2 * hh], ig[:, :, 2 * hh:3 * hh], suf[:, :, 3 * hh:],
                      mst[:, :, :hh], mst[:, :, hh:2 * hh], mst[:, :, 2 * hh:3 * hh], mst[:, :, 3 * hh:]],
                     axis=3)
    pack = pack.transpose(0, 2, 1, 3, 4)
    y_a = _mlstm(q, k, kt, qkv, og, pack, a_norm_g, a_heads, dk, dv, 2 * qk_width)

    y_b = _natten(proj_b, b_rpb, b_q_gain, b_k_gain, b_heads, hd, 0)

    assert a_width == b_width
    x1 = _out_proj(y_a, 0, y_b, 0, w_out0, x, gate0, "layer0_out_proj")

    h1 = _norm_mod(x1, norm_g1, scale1, shift1).reshape(b * s, d)
    uz = _proj(h1, w_in1, [0, 2 * cw], cw, _tile(cw, 512), _uz_gate, BF16, "gmlp_in_proj_uz")
    gv = _proj(h1, w_in1, [cw], cw, _tile(cw, 1024), jax.nn.gelu, BF16, "gmlp_in_proj_v")
    y = _spatial(gv.reshape(b, s, cw), uz.reshape(b, s, cw), c_w_s.astype(BF16), c_b_s.T, c_v_norm_g)
    return _out_proj(y, 0, y, 1, w_out1, x1, gate1, "layer1_out_proj")
```

```python
import functools

import jax
import jax.numpy as jnp
from jax import lax
from jax.experimental import pallas as pl
from jax.experimental.pallas import tpu as pltpu

F32 = jnp.float32
BF16 = jnp.bfloat16
HIGHEST = lax.Precision.HIGHEST

EPS = 1e-6
GRID_W = 64
MLSTM_CHUNK = 256
SPATIAL_CHUNK = 128
LANES = 128
V7X_VMEM_LIMIT_BYTES = 52 * 1024 * 1024


def _params(n_axes, vmem=V7X_VMEM_LIMIT_BYTES):
    return pltpu.CompilerParams(dimension_semantics=("arbitrary",) * n_axes, vmem_limit_bytes=vmem)


def _tile(dim, pref):
    t = min(dim, pref)
    assert dim % t == 0, (dim, pref)
    return t


LOG2E = 1.4426950408889634
GELU_C1 = 2.0 * 0.7978845608028654
GELU_C2 = GELU_C1 * 0.044715


def _silu(x):
    return x * jax.nn.sigmoid(x)


def _silu_exp2(x):
    return x * (1.0 / (1.0 + jnp.exp2(x * (-LOG2E))))


def _gelu_tanh(x):
    t = x * ((-LOG2E * GELU_C1) + (-LOG2E * GELU_C2) * (x * x))
    return x * (1.0 / (1.0 + jnp.exp2(t)))


def _rms(x):
    return x * lax.rsqrt(jnp.mean(x * x, axis=-1, keepdims=True) + EPS)


def _ada_kernel(c_ref, w_ref, b_ref, o_ref):
    a = _silu(c_ref[...]).astype(BF16)
    o_ref[...] = jnp.dot(a, w_ref[...].astype(BF16), preferred_element_type=F32) + b_ref[...]


def _modulation(c_pad, w, b):
    rows, d = c_pad.shape
    n = w.shape[1]
    tn = _tile(n, 512)
    return pl.pallas_call(
        _ada_kernel,
        grid=(n // tn,),
        in_specs=[pl.BlockSpec((rows, d), lambda j: (0, 0)),
                  pl.BlockSpec((d, tn), lambda j: (0, j)),
                  pl.BlockSpec((1, tn), lambda j: (0, j))],
        out_specs=pl.BlockSpec((rows, tn), lambda j: (0, j)),
        out_shape=jax.ShapeDtypeStruct((rows, n), F32),
        compiler_params=_params(1),
        name="ada_modulation",
    )(c_pad, w, b.reshape(1, n))


def _norm_mod_kernel(x_ref, g_ref, sc_ref, sh_ref, *refs):
    y = _rms(x_ref[0]) * g_ref[...]
    h = (y * (1.0 + sc_ref[0]) + sh_ref[0]).astype(BF16)
    if len(refs) == 1:
        refs[0][0] = h
        return
    wt_ref, o_ref, p_ref = refs
    o_ref[0] = h
    p_ref[0] = lax.dot_general(h, wt_ref[...].astype(BF16), _dot_dims(True), preferred_element_type=F32)


def _norm_mod(x, g, scale, shift, wt=None, wt_row0=0):
    b, s, d = x.shape
    ts = _tile(s, 512)
    vec = pl.BlockSpec((1, 1, d), lambda i, j: (i, 0, 0))
    row = pl.BlockSpec((1, ts, d), lambda i, j: (i, j, 0))
    in_specs = [row, pl.BlockSpec((1, d), lambda i, j: (0, 0)), vec, vec]
    out_specs, out_shape = row, jax.ShapeDtypeStruct((b, s, d), BF16)
    args = [x, g.reshape(1, d), scale.reshape(b, 1, d), shift.reshape(b, 1, d)]
    if wt is not None:
        assert wt_row0 % LANES == 0
        in_specs.append(pl.BlockSpec((LANES, d), lambda i, j: (wt_row0 // LANES, 0)))
        out_specs = [row, pl.BlockSpec((1, ts, LANES), lambda i, j: (i, j, 0))]
        out_shape = [out_shape, jax.ShapeDtypeStruct((b, s, LANES), F32)]
        args.append(wt)
    return pl.pallas_call(
        _norm_mod_kernel,
        grid=(b, s // ts),
        in_specs=in_specs,
        out_specs=out_specs,
        out_shape=out_shape,
        compiler_params=_params(2),
        name="norm_modulate",
    )(*args)


BF16_SUBLANES = 16


F32_SUBLANES = 8


def _dot_dims(w_transposed):
    return (((1,), (1,)), ((), ())) if w_transposed else (((1,), (0,)), ((), ()))


def _proj_head_kernel(a_ref, *refs, n_w, w_transposed, epilogue):
    w_refs, o_ref, wb_refs = refs[:n_w], refs[n_w], refs[n_w + 1:]
    a = a_ref[...]
    accs = []
    for w_ref, wb_ref in zip(w_refs, wb_refs):
        wb = w_ref[...].astype(BF16)
        wb_ref[...] = wb
        accs.append(lax.dot_general(a, wb, _dot_dims(w_transposed), preferred_element_type=F32))
    o_ref[...] = epilogue(*accs).astype(o_ref.dtype)


def _proj_tail_kernel(a_ref, *refs, n_w, w_transposed, epilogue):
    w_refs, o_ref = refs[:n_w], refs[-1]
    a = a_ref[...]
    accs = [lax.dot_general(a, w_ref[...], _dot_dims(w_transposed), preferred_element_type=F32)
            for w_ref in w_refs]
    o_ref[...] = epilogue(*accs).astype(o_ref.dtype)


def _proj(a, w, offsets, n, tn, epilogue, out_dtype, name, w_transposed=False):
    m, k = a.shape
    n_w = len(offsets)
    tm = _tile(m, 1024)
    assert n % tn == 0 and m % tm == 0
    tn_head = _tile(tn, 512 // n_w)

    def head_w_spec(off):
        if not w_transposed:
            assert off % tn_head == 0
            return pl.BlockSpec((k, tn_head), lambda j: (0, off // tn_head + j))
        sub = F32_SUBLANES
        assert off % sub == 0 and tn_head % sub == 0
        return pl.BlockSpec((pl.Element(tn_head), pl.Element(k)),
                            lambda j: ((off // sub + j * (tn_head // sub)) * sub, 0))

    if w_transposed:
        wb_shape, wb_head, wb_tail = (n, k), pl.BlockSpec((tn_head, k), lambda j: (j, 0)), \
            pl.BlockSpec((tn, k), lambda i, j: (j, 0))
    else:
        wb_shape, wb_head, wb_tail = (k, n), pl.BlockSpec((k, tn_head), lambda j: (0, j)), \
            pl.BlockSpec((k, tn), lambda i, j: (0, j))
    out_shape = jax.ShapeDtypeStruct((m, n), out_dtype)
    head = pl.pallas_call(
        functools.partial(_proj_head_kernel, n_w=n_w, w_transposed=w_transposed, epilogue=epilogue),
        grid=(n // tn_head,),
        in_specs=[pl.BlockSpec((tm, k), lambda j: (0, 0))] + [head_w_spec(off) for off in offsets],
        out_specs=[pl.BlockSpec((tm, tn_head), lambda j: (0, j))] + [wb_head] * n_w,
        out_shape=[out_shape] + [jax.ShapeDtypeStruct(wb_shape, BF16)] * n_w,
        compiler_params=_params(1),
        name=name + "_head",
    )(a, *([w] * n_w))
    out, wbs = head[0], head[1:]
    if m == tm:
        return out
    return pl.pallas_call(
        functools.partial(_proj_tail_kernel, n_w=n_w, w_transposed=w_transposed, epilogue=epilogue),
        grid=(m // tm - 1, n // tn),
        in_specs=[pl.BlockSpec((tm, k), lambda i, j: (i + 1, 0))] + [wb_tail] * n_w
        + [pl.BlockSpec(memory_space=pl.ANY)],
        out_specs=pl.BlockSpec((tm, tn), lambda i, j: (i + 1, j)),
        out_shape=out_shape,
        input_output_aliases={1 + n_w: 0},
        compiler_params=_params(2),
        name=name,
    )(a, *wbs, out)


def _identity(x):
    return x


def _og_gate(o, z):
    return _silu_exp2(z) * (1.0 / (1.0 + jnp.exp2(o * (-LOG2E))))


def _uz_gate(u, z):
    return _gelu_tanh(u) * _silu_exp2(z)


CONV_ROWS = 512
CONV_HALO = BF16_SUBLANES


def _conv3_silu_rows(x_ref, w, r0, rows):
    s = x_ref.shape[1]
    lo, hi = max(r0 - CONV_HALO, 0), min(r0 + rows + CONV_HALO, s)
    x = x_ref[0, lo:hi, :].astype(F32)
    n = hi - lo
    x_prev, x_next = pltpu.roll(x, 1, axis=0), pltpu.roll(x, n - 1, axis=0)
    row = lax.broadcasted_iota(jnp.int32, x.shape, 0)
    if lo == 0:
        x_prev = jnp.where(row == 0, 0.0, x_prev)
    if hi == s:
        x_next = jnp.where(row == n - 1, 0.0, x_next)
    y = _silu(x_prev * w[0:1] + x * w[1:2] + x_next * w[2:3])
    return y[r0 - lo:r0 - lo + rows]


def _conv_kernel(xq_ref, xk_ref, wq_ref, wk_ref, q_ref, k_ref, kt_ref, *, q_scale):
    s = xq_ref.shape[1]
    l = kt_ref.shape[3]
    rows = min(CONV_ROWS, s)
    wq, wk = wq_ref[...], wk_ref[...]
    for r0 in range(0, s, rows):
        q_ref[0, r0:r0 + rows, :] = (_conv3_silu_rows(xq_ref, wq, r0, rows) * q_scale).astype(BF16)
        k = _conv3_silu_rows(xk_ref, wk, r0, rows)
        k_ref[0, r0:r0 + rows, :] = k.astype(BF16)
        for c in range(rows // l):
            kt_ref[0, r0 // l + c] = k[c * l:(c + 1) * l, :].T.astype(BF16)


def _conv_silu(proj, conv_w, qk_width, dk):
    b, s, _ = proj.shape
    assert conv_w.shape[0] == 3, "centred depthwise conv is written for 3 taps"
    assert s % min(CONV_ROWS, s) == 0 and min(CONV_ROWS, s) % MLSTM_CHUNK == 0
    tc = _tile(qk_width, 256)
    nq = qk_width // tc
    l = MLSTM_CHUNK
    kern = functools.partial(_conv_kernel, q_scale=float(dk) ** -0.5)
    col = pl.BlockSpec((1, s, tc), lambda i, j: (i, 0, j))
    return pl.pallas_call(
        kern,
        grid=(b, nq),
        in_specs=[col, pl.BlockSpec((1, s, tc), lambda i, j: (i, 0, nq + j)),
                  pl.BlockSpec((3, tc), lambda i, j: (0, j)), pl.BlockSpec((3, tc), lambda i, j: (0, nq + j))],
        out_specs=[col, col, pl.BlockSpec((1, s // l, tc, l), lambda i, j: (i, 0, j, 0))],
        out_shape=[jax.ShapeDtypeStruct((b, s, qk_width), BF16), jax.ShapeDtypeStruct((b, s, qk_width), BF16),
                   jax.ShapeDtypeStruct((b, s // l, qk_width, l), BF16)],
        compiler_params=_params(2),
        name="qk_conv_silu",
    )(proj, proj, conv_w, conv_w)


def _gates_kernel(x_ref, b_ref, ig_ref, pre_ref, suf_ref, m_ref):
    nc, r, l = x_ref.shape[1:]
    hh = r // 4
    x = x_ref[0] + b_ref[...][None]
    ig_ref[0] = x
    lf = (jnp.minimum(x, 0.0) - jnp.log1p(jnp.exp(-jnp.abs(x)))).reshape(nc * r, l)
    u = lax.broadcasted_iota(jnp.int32, (l, l), 0)
    t = lax.broadcasted_iota(jnp.int32, (l, l), 1)
    pre = jnp.dot(lf, (u <= t).astype(F32), precision=HIGHEST, preferred_element_type=F32).reshape(nc, r, l)
    suf = jnp.dot(lf, (u >= t).astype(F32), precision=HIGHEST, preferred_element_type=F32).reshape(nc, r, l)
    pre_ref[0] = pre
    suf_ref[0] = suf
    g_f, g_b = pre[:, hh:2 * hh], suf[:, 3 * hh:]
    tot_f, tot_b = g_f[:, :, l - 1:l], g_b[:, :, 0:1]
    a_f = jnp.max(tot_f - g_f + x[:, :hh], axis=2, keepdims=True)
    a_b = jnp.max(tot_b - g_b + x[:, 2 * hh:3 * hh], axis=2, keepdims=True)
    m = jnp.zeros((hh, 1), F32)
    for c in range(nc):
        m_ref[0, c, 0:hh] = jnp.broadcast_to(m, (hh, l))
        m = jnp.maximum(tot_f[c] + m, a_f[c])
        m_ref[0, c, hh:2 * hh] = jnp.broadcast_to(m, (hh, l))
    m = jnp.zeros((hh, 1), F32)
    for c in reversed(range(nc)):
        m_ref[0, c, 2 * hh:3 * hh] = jnp.broadcast_to(m, (hh, l))
        m = jnp.maximum(tot_b[c] + m, a_b[c])
        m_ref[0, c, 3 * hh:] = jnp.broadcast_to(m, (hh, l))


def _gate_prep(graw, bias):
    b, nc, r, l = graw.shape
    spec = pl.BlockSpec((1, nc, r, l), lambda i: (i, 0, 0, 0))
    shape = jax.ShapeDtypeStruct(graw.shape, F32)
    return pl.pallas_call(
        _gates_kernel,
        grid=(b,),
        in_specs=[spec, pl.BlockSpec((r, 1), lambda i: (0, 0))],
        out_specs=[spec] * 4,
        out_shape=[shape] * 4,
        compiler_params=_params(1),
        name="mlstm_gate_prep",
    )(graw, bias.reshape(r, 1))


def _row_to_col(row_vec, eye):
    return jnp.sum(jnp.where(eye, row_vec, 0.0), axis=1, keepdims=True)


def _mlstm_local(q, kt, ig, g, tot, m, m_new, *, causal):
    l = q.shape[0]
    r_i = lax.broadcasted_iota(jnp.int32, (l, l), 0)
    c_i = lax.broadcasted_iota(jnp.int32, (l, l), 1)
    g_col = _row_to_col(g, r_i == c_i)
    d = g_col - (g - ig)
    d = jnp.where((c_i <= r_i) if causal else (c_i >= r_i), d, -jnp.inf)
    gm = g_col + m
    m_t = jnp.maximum(gm, jnp.max(d, axis=1, keepdims=True))
    s_ = jnp.dot(q, kt, preferred_element_type=F32) * jnp.exp(d - m_t)
    w = jnp.exp(tot - g + ig - m_new)
    return dict(
        s=s_.astype(BF16), s_sum=jnp.sum(s_, axis=1, keepdims=True),
        inter=jnp.exp(gm - m_t), floor=jnp.exp(-m_t), decay=jnp.exp(tot + m - m_new),
        wkt=(kt.astype(F32) * w).astype(BF16),
        w_rows=jnp.broadcast_to(w, (8, l)).astype(BF16))


def _mlstm_state(q, k, v, loc, c, n):
    qc = jnp.dot(q, c.astype(BF16), preferred_element_type=F32)
    num = jnp.dot(loc["s"], v, preferred_element_type=F32) + loc["inter"] * qc
    qn = jnp.sum(q.astype(F32) * n, axis=1, keepdims=True)
    den = loc["s_sum"] + loc["inter"] * qn
    h = num * (1.0 / jnp.maximum(jnp.abs(den), loc["floor"]))
    c_new = loc["decay"] * c + jnp.dot(loc["wkt"], v, preferred_element_type=F32)
    n_new = loc["decay"] * n + jnp.dot(loc["w_rows"], k, preferred_element_type=F32)[0:1]
    return h, c_new, n_new


MLSTM_UNROLL = 2


def _mlstm_kernel(q_ref, k_ref, kt_ref, v_ref, og_ref, gp_ref, ng_ref, o_ref, acc_ref, cf_ref, cb_ref):
    s, dv = v_ref.shape[1:]
    dk = q_ref.shape[2]
    l = MLSTM_CHUNK
    nc = s // l
    cf_ref[...] = jnp.zeros_like(cf_ref)
    cb_ref[...] = jnp.zeros_like(cb_ref)
    gain = ng_ref[0]

    def rows_of(c):
        return pl.ds(pl.multiple_of(c * l, l), l)

    def local(c, causal):
        gp = gp_ref[0, 0, c]
        if causal:
            ig, g, m, m_new = gp[0:1], gp[1:2], gp[4:5, 0:1], gp[5:6, 0:1]
            tot = g[:, l - 1:l]
        else:
            ig, g, m, m_new = gp[2:3], gp[3:4], gp[6:7, 0:1], gp[7:8, 0:1]
            tot = g[:, 0:1]
        return _mlstm_local(q_ref[0, rows_of(c), :], kt_ref[0, c], ig, g, tot, m, m_new, causal=causal)

    def emit(rows, h, first_touch):
        if first_touch:
            acc_ref[rows, :] = h
        else:
            y = _rms(acc_ref[rows, :] + h) * gain * og_ref[0, rows, :].astype(F32)
            o_ref[0, rows, :] = y.astype(o_ref.dtype)

    def body(first_touch, i, carry):
        chunks = [(i * MLSTM_UNROLL + u, nc - 1 - (i * MLSTM_UNROLL + u)) for u in range(MLSTM_UNROLL)]
        n_f, n_b = carry
        c_f, c_b = cf_ref[...], cb_ref[...]
        for cf, cb in chunks:
            rows = rows_of(cf)
            h, c_f, n_f = _mlstm_state(q_ref[0, rows, :], k_ref[0, rows, :], v_ref[0, rows, :],
                                       local(cf, True), c_f, n_f)
            emit(rows, h, first_touch)
            rows = rows_of(cb)
            h, c_b, n_b = _mlstm_state(q_ref[0, rows, :], k_ref[0, rows, :], v_ref[0, rows, :],
                                       local(cb, False), c_b, n_b)
            emit(rows, h, first_touch)
        cf_ref[...] = c_f
        cb_ref[...] = c_b
        return n_f, n_b

    n0 = jnp.zeros((1, dk), F32)
    n_it = nc // MLSTM_UNROLL
    carry = lax.fori_loop(0, n_it // 2, functools.partial(body, True), (n0, n0))
    lax.fori_loop(n_it // 2, n_it, functools.partial(body, False), carry)


def _mlstm(q, k, kt, qkv, og, gate_pack, norm_g, heads, dk, dv, v_col0):
    b, s, _ = qkv.shape
    assert s % (2 * MLSTM_UNROLL * MLSTM_CHUNK) == 0 and v_col0 % dv == 0
    nc = s // MLSTM_CHUNK
    seq_dk = pl.BlockSpec((1, s, dk), lambda i, h: (i, 0, h))
    seq_dv = pl.BlockSpec((1, s, dv), lambda i, h: (i, 0, h))
    return pl.pallas_call(
        _mlstm_kernel,
        grid=(b, heads),
        in_specs=[seq_dk, seq_dk,
                  pl.BlockSpec((1, nc, dk, MLSTM_CHUNK), lambda i, h: (i, 0, h, 0)),
                  pl.BlockSpec((1, s, dv), lambda i, h: (i, 0, v_col0 // dv + h)),
                  seq_dv,
                  pl.BlockSpec((1, 1, nc, 8, MLSTM_CHUNK), lambda i, h: (i, h, 0, 0, 0)),
                  pl.BlockSpec((1, 1, dv), lambda i, h: (h, 0, 0))],
        out_specs=seq_dv,
        out_shape=jax.ShapeDtypeStruct((b, s, heads * dv), BF16),
        scratch_shapes=[pltpu.VMEM((s, dv), F32),
                        pltpu.VMEM((dk, dv), F32),
                        pltpu.VMEM((dk, dv), F32)],
        compiler_params=_params(2),
        name="mlstm_bidirectional",
    )(q, k, kt, qkv, og, gate_pack, norm_g.reshape(heads, 1, dv))


def _rpb_toeplitz_kernel(r_ref, o_ref, *, win_c):
    dpad, n = r_ref.shape[1], o_ref.shape[1]
    dd = lax.broadcasted_iota(jnp.int32, (dpad, n), 0)
    cc = lax.broadcasted_iota(jnp.int32, (dpad, n), 1)
    shift = GRID_W.bit_length() - 1
    qc, kc = cc >> shift, cc & (GRID_W - 1)
    dc = jnp.clip(kc - qc + (win_c - 1), 0, 2 * win_c - 2)
    onehot = jnp.where(dd == dc, 1.0, 0.0).astype(F32)
    vals = jnp.dot(r_ref[...], onehot, precision=HIGHEST, preferred_element_type=F32) * LOG2E
    co = lax.broadcasted_iota(jnp.int32, (1, n), 1)
    qo, ko = co >> shift, co & (GRID_W - 1)
    cs = jnp.clip(qo - win_c // 2, 0, GRID_W - win_c)
    o_ref[...] = jnp.where((ko >= cs) & (ko < cs + win_c), vals, -jnp.inf)


def _natten_plan(n_rows, win_r, q_rows, k_rows):
    variants, var_idx, k_start = [], [], []
    for r0 in range(0, n_rows, q_rows):
        kp = min(max(r0 - win_r // 2, 0), n_rows - k_rows)
        pattern = []
        for i in range(q_rows):
            r = r0 + i
            rs = min(max(r - win_r // 2, 0), n_rows - win_r)
            assert kp <= rs and rs + win_r <= kp + k_rows
            pattern.append(tuple((kp + jj - r + win_r - 1) if rs <= kp + jj < rs + win_r else None
                                 for jj in range(k_rows)))
        pattern = tuple(pattern)
        if pattern not in variants:
            variants.append(pattern)
        var_idx.append(variants.index(pattern))
        k_start.append(kp)
    return variants, var_idx, k_start


def _bias_assemble_kernel(t_ref, o_ref, *, variants):
    w = GRID_W
    masked = jnp.full((w, w), -jnp.inf, F32)
    for v, pattern in enumerate(variants):
        for i, row in enumerate(pattern):
            tiles = [masked if dr is None else t_ref[0, dr] for dr in row]
            o_ref[0, v, i * w:(i + 1) * w, :] = jnp.concatenate(tiles, axis=1)


def _natten_bias(rpb, variants):
    hb, n_dr, n_dc = rpb.shape
    win_c = (n_dc + 1) // 2
    w = GRID_W
    dpad = 32
    assert n_dc <= dpad and (hb * n_dr) % 8 == 0
    r = jnp.pad(rpb, ((0, 0), (0, 0), (0, dpad - n_dc))).reshape(hb * n_dr, dpad)
    toeplitz = pl.pallas_call(
        functools.partial(_rpb_toeplitz_kernel, win_c=win_c),
        grid=(1,),
        in_specs=[pl.BlockSpec(r.shape, lambda i: (0, 0))],
        out_specs=pl.BlockSpec((r.shape[0], w * w), lambda i: (0, 0)),
        out_shape=jax.ShapeDtypeStruct((r.shape[0], w * w), F32),
        compiler_params=_params(1),
        name="natten_rpb_toeplitz",
    )(r).reshape(hb, n_dr, w, w)
    q_rows, k_rows = len(variants[0]), len(variants[0][0])
    out_blk = (1, len(variants), q_rows * w, k_rows * w)
    return pl.pallas_call(
        functools.partial(_bias_assemble_kernel, variants=variants),
        grid=(hb,),
        in_specs=[pl.BlockSpec((1, n_dr, w, w), lambda h: (h, 0, 0, 0))],
        out_specs=pl.BlockSpec(out_blk, lambda h: (h, 0, 0, 0)),
        out_shape=jax.ShapeDtypeStruct((hb,) + out_blk[1:], F32),
        compiler_params=_params(1),
        name="natten_bias_assemble",
    )(toeplitz)


NATTEN_Q_ROWS = 1
NATTEN_BATCH = 16


def _natten_kernel(var_ref, ks_ref, q_ref, k_ref, v_ref, z_ref, bias_ref, qg_ref, kg_ref, o_ref,
                   qn_ref, kn_ref, sc_a, sc_b):
    s, hd = q_ref.shape[1:]
    w = GRID_W
    nq, nk = bias_ref.shape[2:]
    n_batches = s // (nq * NATTEN_BATCH)
    def unit_rows(x):
        return x * lax.rsqrt(jnp.sum(x * x, axis=-1, keepdims=True) + hd * EPS)

    qn_ref[...] = (unit_rows(q_ref[0].astype(F32)) * (qg_ref[...] * LOG2E)).astype(BF16)
    kn_ref[...] = (unit_rows(k_ref[0].astype(F32)) * (kg_ref[...] * (float(hd) ** 0.5))).astype(BF16)

    def rows_of(p):
        return (pl.ds(pl.multiple_of(p * nq, nq), nq), pl.ds(pl.multiple_of(ks_ref[p] * w, w), nk))

    def scores(p):
        q_rows, kv_rows = rows_of(p)
        qk = lax.dot_general(qn_ref[q_rows, :], kn_ref[kv_rows, :], (((1,), (1,)), ((), ())),
                             preferred_element_type=F32)
        return qk + bias_ref[0, var_ref[p]]

    def attend(p, sc):
        q_rows, kv_rows = rows_of(p)
        p_un = jnp.exp2(sc - jnp.max(sc, axis=1, keepdims=True))
        o = jnp.dot(p_un.astype(BF16), v_ref[0, kv_rows, :], preferred_element_type=F32)
        o = o * (1.0 / jnp.sum(p_un, axis=1, keepdims=True))
        o_ref[0, q_rows, :] = (o * _silu(z_ref[0, q_rows, :].astype(F32))).astype(o_ref.dtype)

    def fill(buf, j):
        for u in range(NATTEN_BATCH):
            buf[u] = scores(j * NATTEN_BATCH + u)

    def drain(buf, j):
        for u in range(NATTEN_BATCH):
            attend(j * NATTEN_BATCH + u, buf[u])

    def body(k, carry):
        fill(sc_b, 2 * k + 1)
        drain(sc_a, 2 * k)
        fill(sc_a, 2 * k + 2)
        drain(sc_b, 2 * k + 1)
        return carry

    fill(sc_a, 0)
    lax.fori_loop(0, n_batches // 2 - 1, body, 0)
    fill(sc_b, n_batches - 1)
    drain(sc_a, n_batches - 2)
    drain(sc_b, n_batches - 1)


def _natten(proj, rpb, q_gain, k_gain, heads, hd, col0):
    b, s, _ = proj.shape
    win_r = (rpb.shape[1] + 1) // 2
    n_rows = s // GRID_W
    k_rows = win_r + NATTEN_Q_ROWS - 1
    k_rows += k_rows % 2
    assert s % GRID_W == 0 and n_rows >= k_rows and col0 % hd == 0
    assert n_rows % (2 * NATTEN_Q_ROWS * NATTEN_BATCH) == 0
    variants, var_idx, k_start = _natten_plan(n_rows, win_r, NATTEN_Q_ROWS, k_rows)
    bias = _natten_bias(rpb, variants)
    sc_buf = pltpu.VMEM((NATTEN_BATCH,) + bias.shape[2:], F32)
    c0 = col0 // hd

    def col(part):
        return pl.BlockSpec((1, s, hd), lambda h, i: (i, 0, c0 + part * heads + h))

    smem = pl.BlockSpec(memory_space=pltpu.SMEM)
    vec = pl.BlockSpec((1, hd), lambda h, i: (0, 0))
    return pl.pallas_call(
        _natten_kernel,
        grid=(heads, b),
        in_specs=[smem, smem, col(0), col(1), col(2), col(3),
                  pl.BlockSpec((1,) + bias.shape[1:], lambda h, i: (h, 0, 0, 0)), vec, vec],
        out_specs=pl.BlockSpec((1, s, hd), lambda h, i: (i, 0, h)),
        out_shape=jax.ShapeDtypeStruct((b, s, heads * hd), BF16),
        scratch_shapes=[pltpu.VMEM((s, hd), BF16), pltpu.VMEM((s, hd), BF16), sc_buf, sc_buf],
        compiler_params=_params(2),
        name="neighbourhood_attention",
    )(jnp.asarray(var_idx, jnp.int32), jnp.asarray(k_start, jnp.int32), proj, proj, proj, proj, bias,
      q_gain.reshape(1, hd), k_gain.reshape(1, hd))


def _mm_out_head_kernel(a1_ref, a2_ref, w1_ref, w2_ref, x_ref, gate_ref, o_ref, wb1_ref, wb2_ref):
    wb1, wb2 = w1_ref[...].astype(BF16), w2_ref[...].astype(BF16)
    wb1_ref[...] = wb1
    wb2_ref[...] = wb2
    acc = jnp.dot(a1_ref[0], wb1, preferred_element_type=F32)
    acc = acc + jnp.dot(a2_ref[0], wb2, preferred_element_type=F32)
    o_ref[0] = x_ref[0] + gate_ref[0] * acc


def _mm_out_kernel(a1_ref, a2_ref, w1_ref, w2_ref, x_ref, gate_ref, _, o_ref):
    acc = jnp.dot(a1_ref[0], w1_ref[...], preferred_element_type=F32)
    acc = acc + jnp.dot(a2_ref[0], w2_ref[...], preferred_element_type=F32)
    o_ref[0] = x_ref[0] + gate_ref[0] * acc


def _out_proj(a1, a1_blk, a2, a2_blk, w, x, gate, name):
    b, s, d = x.shape
    kh = w.shape[0] // 2
    tm, tn = _tile(s, 512), _tile(d, 512)
    tn_head = _tile(tn, 256)
    mt = s // tm
    gate3 = gate.reshape(b, 1, d)
    out_shape = jax.ShapeDtypeStruct((b, s, d), F32)
    wb_shape = jax.ShapeDtypeStruct((kh, d), BF16)
    out, wb1, wb2 = pl.pallas_call(
        _mm_out_head_kernel,
        grid=(d // tn_head,),
        in_specs=[pl.BlockSpec((1, tm, kh), lambda n: (0, 0, a1_blk)),
                  pl.BlockSpec((1, tm, kh), lambda n: (0, 0, a2_blk)),
                  pl.BlockSpec((kh, tn_head), lambda n: (0, n)),
                  pl.BlockSpec((kh, tn_head), lambda n: (1, n)),
                  pl.BlockSpec((1, tm, tn_head), lambda n: (0, 0, n)),
                  pl.BlockSpec((1, 1, tn_head), lambda n: (0, 0, n))],
        out_specs=[pl.BlockSpec((1, tm, tn_head), lambda n: (0, 0, n)),
                   pl.BlockSpec((kh, tn_head), lambda n: (0, n)),
                   pl.BlockSpec((kh, tn_head), lambda n: (0, n))],
        out_shape=[out_shape, wb_shape, wb_shape],
        compiler_params=_params(1),
        name=name + "_head",
    )(a1, a2, w, w, x, gate3)
    if b * mt == 1:
        return out

    def bm(t):
        return (t + 1) // mt, (t + 1) % mt

    return pl.pallas_call(
        _mm_out_kernel,
        grid=(b * mt - 1, d // tn),
        in_specs=[pl.BlockSpec((1, tm, kh), lambda t, n: (*bm(t), a1_blk)),
                  pl.BlockSpec((1, tm, kh), lambda t, n: (*bm(t), a2_blk)),
                  pl.BlockSpec((kh, tn), lambda t, n: (0, n)),
                  pl.BlockSpec((kh, tn), lambda t, n: (0, n)),
                  pl.BlockSpec((1, tm, tn), lambda t, n: (*bm(t), n)),
                  pl.BlockSpec((1, 1, tn), lambda t, n: (bm(t)[0], 0, n)),
                  pl.BlockSpec(memory_space=pl.ANY)],
        out_specs=pl.BlockSpec((1, tm, tn), lambda t, n: (*bm(t), n)),
        out_shape=out_shape,
        input_output_aliases={6: 0},
        compiler_params=_params(2),
        name=name,
    )(a1, a2, wb1, wb2, x, gate3, out)


def _spatial_kernel(gv_ref, uz_ref, ws_ref, bs_ref, vg_ref, o_ref):
    groups, l = ws_ref.shape[:2]
    cg = gv_ref.shape[2] // groups
    for r0 in range(0, gv_ref.shape[1], l):
        rows = slice(r0, r0 + l)
        vn = (_rms(gv_ref[0, rows, :].astype(F32)) * vg_ref[...]).astype(BF16)
        for g in range(groups):
            cols = slice(g * cg, (g + 1) * cg)
            sv = jnp.dot(ws_ref[g], vn[:, cols], preferred_element_type=F32) + bs_ref[:, g:g + 1]
            o_ref[0, rows, cols] = (uz_ref[0, rows, cols].astype(F32) * sv).astype(BF16)


SPATIAL_CHUNKS_PER_STEP = 2


def _spatial(gv, uz, ws, bs_t, v_gain):
    b, s, cw = gv.shape
    l = SPATIAL_CHUNK
    groups = ws.shape[0]
    assert s % l == 0 and ws.shape[1:] == (l, l)
    rows = _tile(s, l * SPATIAL_CHUNKS_PER_STEP)
    blk = pl.BlockSpec((1, rows, cw), lambda i, n: (i, n, 0))
    return pl.pallas_call(
        _spatial_kernel,
        grid=(b, s // rows),
        in_specs=[blk, blk,
                  pl.BlockSpec(ws.shape, lambda i, n: (0, 0, 0)),
                  pl.BlockSpec((l, groups), lambda i, n: (0, 0)),
                  pl.BlockSpec((1, cw), lambda i, n: (0, 0))],
        out_specs=blk,
        out_shape=jax.ShapeDtypeStruct((b, s, cw), BF16),
        compiler_params=_params(2),
        name="gmlp_spatial_gate",
    )(gv, uz, ws, bs_t, v_gain.reshape(1, cw))


def kernel(x, c, norm_g0, ada_w0, ada_b0, w_in0, a_conv_w, a_gate_b, a_norm_g, b_q_gain, b_k_gain, b_rpb, w_out0,
           norm_g1, ada_w1, ada_b1, w_in1, c_v_norm_g, c_w_s, c_b_s, w_out1):
    b, s, d = x.shape
    a_heads, dv = a_norm_g.shape
    a_width = a_heads * dv
    dk = dv // 2
    qk_width = a_heads * dk
    hd = b_q_gain.shape[0]
    b_heads, b_width = b_rpb.shape[0], b_rpb.shape[0] * hd
    n_gate = 4 * a_heads
    a_cols = 2 * qk_width + 3 * a_width
    assert w_in0.shape[1] == a_cols + n_gate + 4 * b_width
    cw = c_v_norm_g.shape[0]
    nc = s // MLSTM_CHUNK

    c_pad = jnp.pad(c, ((0, -b % 8), (0, 0)))
    mod0 = _modulation(c_pad, ada_w0, ada_b0)[:b]
    mod1 = _modulation(c_pad, ada_w1, ada_b1)[:b]
    shift0, scale0, gate0 = mod0[:, :d], mod0[:, d:2 * d], mod0[:, 2 * d:]
    shift1, scale1, gate1 = mod1[:, :d], mod1[:, d:2 * d], mod1[:, 2 * d:]

    qkv_cols = 2 * qk_width + a_width
    wt0 = w_in0.T
    assert a_cols % LANES == 0 and n_gate <= LANES
    h0, gates = _norm_mod(x, norm_g0, scale0, shift0, wt0, a_cols)
    h0, gates = h0.reshape(b * s, d), gates.reshape(b * s, LANES)[:, :n_gate]
    qkv = _proj(h0, wt0, [0], qkv_cols, _tile(qkv_cols, 1024), _identity, BF16, "layer0_in_proj_qkv",
                w_transposed=True).reshape(b, s, qkv_cols)
    og = _proj(h0, wt0, [qkv_cols, qkv_cols + a_width], a_width, _tile(a_width, 512), _og_gate, BF16,
               "layer0_in_proj_og", w_transposed=True).reshape(b, s, a_width)
    proj_b = _proj(h0, wt0, [a_cols + n_gate], 4 * b_width, _tile(4 * b_width, 1024), _identity, BF16,
                   "layer0_in_proj_b", w_transposed=True).reshape(b, s, 4 * b_width)

    q, k, kt = _conv_silu(qkv, a_conv_w, qk_width, dk)
    graw = gates.reshape(b, nc, MLSTM_CHUNK, n_gate).transpose(0, 1, 3, 2)
    ig, pre, suf, mst = _gate_prep(graw, a_gate_b)
    hh = a_heads
    pack = jnp.stack([ig[:, :, :hh], pre[:, :, hh:2 * hh], ig[:, :, 2 * hh:3 * hh], suf[:, :, 3 * hh:],
                      mst[:, :, :hh], mst[:, :, hh:2 * hh], mst[:, :, 2 * hh:3 * hh], mst[:, :, 3 * hh:]],
                     axis=3)
    pack = pack.transpose(0, 2, 1, 3, 4)
    y_a = _mlstm(q, k, kt, qkv, og, pack, a_norm_g, a_heads, dk, dv, 2 * qk_width)

    y_b = _natten(proj_b, b_rpb, b_q_gain, b_k_gain, b_heads, hd, 0)

    assert a_width == b_width
    x1 = _out_proj(y_a, 0, y_b, 0, w_out0, x, gate0, "layer0_out_proj")

    h1 = _norm_mod(x1, norm_g1, scale1, shift1).reshape(b * s, d)
    uz = _proj(h1, w_in1, [0, 2 * cw], cw, _tile(cw, 512), _uz_gate, BF16, "gmlp_in_proj_uz")
    gv = _proj(h1, w_in1, [cw], cw, _tile(cw, 1024), _gelu_tanh, BF16, "gmlp_in_proj_v")
    y = _spatial(gv.reshape(b, s, cw), uz.reshape(b, s, cw), c_w_s.astype(BF16), c_b_s.T, c_v_norm_g)
    return _out_proj(y, 0, y, 1, w_out1, x1, gate1, "layer1_out_proj")
```

```python
import functools

import jax
import jax.numpy as jnp
from jax import lax
from jax.experimental import pallas as pl
from jax.experimental.pallas import tpu as pltpu

F32 = jnp.float32
BF16 = jnp.bfloat16
HIGHEST = lax.Precision.HIGHEST

EPS = 1e-6
GRID_W = 64
MLSTM_CHUNK = 256
SPATIAL_CHUNK = 128
LANES = 128
V7X_VMEM_LIMIT_BYTES = 52 * 1024 * 1024


def _params(n_axes, vmem=V7X_VMEM_LIMIT_BYTES):
    return pltpu.CompilerParams(dimension_semantics=("arbitrary",) * n_axes, vmem_limit_bytes=vmem)


def _tile(dim, pref):
    t = min(dim, pref)
    assert dim % t == 0, (dim, pref)
    return t


LOG2E = 1.4426950408889634
GELU_C1 = 2.0 * 0.7978845608028654
GELU_C2 = GELU_C1 * 0.044715


def _silu(x):
    return x * jax.nn.sigmoid(x)


def _silu_exp2(x):
    return x * (1.0 / (1.0 + jnp.exp2(x * (-LOG2E))))


def _gelu_tanh(x):
    t = x * ((-LOG2E * GELU_C1) + (-LOG2E * GELU_C2) * (x * x))
    return x * (1.0 / (1.0 + jnp.exp2(t)))


def _rms(x):
    return x * lax.rsqrt(jnp.mean(x * x, axis=-1, keepdims=True) + EPS)


def _ada_kernel(c_ref, w_ref, b_ref, o_ref):
    a = _silu(c_ref[...]).astype(BF16)
    o_ref[...] = jnp.dot(a, w_ref[...].astype(BF16), preferred_element_type=F32) + b_ref[...]


def _modulation(c_pad, w, b):
    rows, d = c_pad.shape
    n = w.shape[1]
    tn = _tile(n, 512)
    return pl.pallas_call(
        _ada_kernel,
        grid=(n // tn,),
        in_specs=[pl.BlockSpec((rows, d), lambda j: (0, 0)),
                  pl.BlockSpec((d, tn), lambda j: (0, j)),
                  pl.BlockSpec((1, tn), lambda j: (0, j))],
        out_specs=pl.BlockSpec((rows, tn), lambda j: (0, j)),
        out_shape=jax.ShapeDtypeStruct((rows, n), F32),
        compiler_params=_params(1),
        name="ada_modulation",
    )(c_pad, w, b.reshape(1, n))


def _norm_mod_kernel(x_ref, g_ref, sc_ref, sh_ref, *refs):
    y = _rms(x_ref[0]) * g_ref[...]
    h = (y * (1.0 + sc_ref[0]) + sh_ref[0]).astype(BF16)
    if len(refs) == 1:
        refs[0][0] = h
        return
    wt_ref, o_ref, p_ref = refs
    o_ref[0] = h
    p_ref[0] = lax.dot_general(h, wt_ref[...].astype(BF16), _dot_dims(True), preferred_element_type=F32)


def _norm_mod(x, g, scale, shift, wt=None, wt_row0=0):
    b, s, d = x.shape
    ts = _tile(s, 512)
    vec = pl.BlockSpec((1, 1, d), lambda i, j: (i, 0, 0))
    row = pl.BlockSpec((1, ts, d), lambda i, j: (i, j, 0))
    in_specs = [row, pl.BlockSpec((1, d), lambda i, j: (0, 0)), vec, vec]
    out_specs, out_shape = row, jax.ShapeDtypeStruct((b, s, d), BF16)
    args = [x, g.reshape(1, d), scale.reshape(b, 1, d), shift.reshape(b, 1, d)]
    if wt is not None:
        assert wt_row0 % LANES == 0
        in_specs.append(pl.BlockSpec((LANES, d), lambda i, j: (wt_row0 // LANES, 0)))
        out_specs = [row, pl.BlockSpec((1, ts, LANES), lambda i, j: (i, j, 0))]
        out_shape = [out_shape, jax.ShapeDtypeStruct((b, s, LANES), F32)]
        args.append(wt)
    return pl.pallas_call(
        _norm_mod_kernel,
        grid=(b, s // ts),
        in_specs=in_specs,
        out_specs=out_specs,
        out_shape=out_shape,
        compiler_params=_params(2),
        name="norm_modulate",
    )(*args)


BF16_SUBLANES = 16


F32_SUBLANES = 8


def _dot_dims(w_transposed):
    return (((1,), (1,)), ((), ())) if w_transposed else (((1,), (0,)), ((), ()))


def _proj_head_kernel(a_ref, *refs, n_w, w_transposed, epilogue):
    w_refs, o_ref, wb_refs = refs[:n_w], refs[n_w], refs[n_w + 1:]
    a = a_ref[...]
    accs = []
    for w_ref, wb_ref in zip(w_refs, wb_refs):
        wb = w_ref[...].astype(BF16)
        wb_ref[...] = wb
        accs.append(lax.dot_general(a, wb, _dot_dims(w_transposed), preferred_element_type=F32))
    o_ref[...] = epilogue(*accs).astype(o_ref.dtype)


def _proj_tail_kernel(a_ref, *refs, n_w, w_transposed, epilogue):
    w_refs, o_ref = refs[:n_w], refs[-1]
    a = a_ref[...]
    accs = [lax.dot_general(a, w_ref[...], _dot_dims(w_transposed), preferred_element_type=F32)
            for w_ref in w_refs]
    o_ref[...] = epilogue(*accs).astype(o_ref.dtype)


def _proj(a, w, offsets, n, tn, epilogue, out_dtype, name, w_transposed=False):
    m, k = a.shape
    n_w = len(offsets)
    tm = _tile(m, 1024)
    assert n % tn == 0 and m % tm == 0
    tn_head = _tile(tn, 512 // n_w)

    def head_w_spec(off):
        if not w_transposed:
            assert off % tn_head == 0
            return pl.BlockSpec((k, tn_head), lambda j: (0, off // tn_head + j))
        sub = F32_SUBLANES
        assert off % sub == 0 and tn_head % sub == 0
        return pl.BlockSpec((pl.Element(tn_head), pl.Element(k)),
                            lambda j: ((off // sub + j * (tn_head // sub)) * sub, 0))

    if w_transposed:
        wb_shape, wb_head, wb_tail = (n, k), pl.BlockSpec((tn_head, k), lambda j: (j, 0)), \
            pl.BlockSpec((tn, k), lambda i, j: (j, 0))
    else:
        wb_shape, wb_head, wb_tail = (k, n), pl.BlockSpec((k, tn_head), lambda j: (0, j)), \
            pl.BlockSpec((k, tn), lambda i, j: (0, j))
    out_shape = jax.ShapeDtypeStruct((m, n), out_dtype)
    head = pl.pallas_call(
        functools.partial(_proj_head_kernel, n_w=n_w, w_transposed=w_transposed, epilogue=epilogue),
        grid=(n // tn_head,),
        in_specs=[pl.BlockSpec((tm, k), lambda j: (0, 0))] + [head_w_spec(off) for off in offsets],
        out_specs=[pl.BlockSpec((tm, tn_head), lambda j: (0, j))] + [wb_head] * n_w,
        out_shape=[out_shape] + [jax.ShapeDtypeStruct(wb_shape, BF16)] * n_w,
        compiler_params=_params(1),
        name=name + "_head",
    )(a, *([w] * n_w))
    out, wbs = head[0], head[1:]
    if m == tm:
        return out
    return pl.pallas_call(
        functools.partial(_proj_tail_kernel, n_w=n_w, w_transposed=w_transposed, epilogue=epilogue),
        grid=(m // tm - 1, n // tn),
        in_specs=[pl.BlockSpec((tm, k), lambda i, j: (i + 1, 0))] + [wb_tail] * n_w
        + [pl.BlockSpec(memory_space=pl.ANY)],
        out_specs=pl.BlockSpec((tm, tn), lambda i, j: (i + 1, j)),
        out_shape=out_shape,
        input_output_aliases={1 + n_w: 0},
        compiler_params=_params(2),
        name=name,
    )(a, *wbs, out)


def _identity(x):
    return x


def _og_gate(o, z):
    return _silu_exp2(z) * (1.0 / (1.0 + jnp.exp2(o * (-LOG2E))))


def _uz_gate(u, z):
    return _gelu_tanh(u) * _silu_exp2(z)


CONV_ROWS = 512
CONV_HALO = BF16_SUBLANES


def _conv3_silu_rows(x_ref, w, r0, rows):
    s = x_ref.shape[1]
    lo, hi = max(r0 - CONV_HALO, 0), min(r0 + rows + CONV_HALO, s)
    x = x_ref[0, lo:hi, :].astype(F32)
    n = hi - lo
    x_prev, x_next = pltpu.roll(x, 1, axis=0), pltpu.roll(x, n - 1, axis=0)
    row = lax.broadcasted_iota(jnp.int32, x.shape, 0)
    if lo == 0:
        x_prev = jnp.where(row == 0, 0.0, x_prev)
    if hi == s:
        x_next = jnp.where(row == n - 1, 0.0, x_next)
    y = _silu(x_prev * w[0:1] + x * w[1:2] + x_next * w[2:3])
    return y[r0 - lo:r0 - lo + rows]


def _conv_kernel(xq_ref, xk_ref, wq_ref, wk_ref, q_ref, kt_ref, *, q_scale):
    s = xq_ref.shape[1]
    l = kt_ref.shape[3]
    rows = min(CONV_ROWS, s)
    wq, wk = wq_ref[...], wk_ref[...]
    for r0 in range(0, s, rows):
        q_ref[0, r0:r0 + rows, :] = (_conv3_silu_rows(xq_ref, wq, r0, rows) * q_scale).astype(BF16)
        k = _conv3_silu_rows(xk_ref, wk, r0, rows)
        for c in range(rows // l):
            kt_ref[0, r0 // l + c] = k[c * l:(c + 1) * l, :].T.astype(BF16)


def _conv_silu(proj, conv_w, qk_width, dk):
    b, s, _ = proj.shape
    assert conv_w.shape[0] == 3, "centred depthwise conv is written for 3 taps"
    assert s % min(CONV_ROWS, s) == 0 and min(CONV_ROWS, s) % MLSTM_CHUNK == 0
    tc = _tile(qk_width, 256)
    nq = qk_width // tc
    l = MLSTM_CHUNK
    kern = functools.partial(_conv_kernel, q_scale=float(dk) ** -0.5)
    col = pl.BlockSpec((1, s, tc), lambda i, j: (i, 0, j))
    return pl.pallas_call(
        kern,
        grid=(b, nq),
        in_specs=[col, pl.BlockSpec((1, s, tc), lambda i, j: (i, 0, nq + j)),
                  pl.BlockSpec((3, tc), lambda i, j: (0, j)), pl.BlockSpec((3, tc), lambda i, j: (0, nq + j))],
        out_specs=[col, pl.BlockSpec((1, s // l, tc, l), lambda i, j: (i, 0, j, 0))],
        out_shape=[jax.ShapeDtypeStruct((b, s, qk_width), BF16),
                   jax.ShapeDtypeStruct((b, s // l, qk_width, l), BF16)],
        compiler_params=_params(2),
        name="qk_conv_silu",
    )(proj, proj, conv_w, conv_w)


def _gates_kernel(x_ref, b_ref, ig_ref, pre_ref, suf_ref, m_ref):
    nc, r, l = x_ref.shape[1:]
    hh = r // 4
    x = x_ref[0] + b_ref[...][None]
    ig_ref[0] = x
    lf = (jnp.minimum(x, 0.0) - jnp.log1p(jnp.exp(-jnp.abs(x)))).reshape(nc * r, l)
    u = lax.broadcasted_iota(jnp.int32, (l, l), 0)
    t = lax.broadcasted_iota(jnp.int32, (l, l), 1)
    pre = jnp.dot(lf, (u <= t).astype(F32), precision=HIGHEST, preferred_element_type=F32).reshape(nc, r, l)
    suf = jnp.dot(lf, (u >= t).astype(F32), precision=HIGHEST, preferred_element_type=F32).reshape(nc, r, l)
    pre_ref[0] = pre
    suf_ref[0] = suf
    g_f, g_b = pre[:, hh:2 * hh], suf[:, 3 * hh:]
    tot_f, tot_b = g_f[:, :, l - 1:l], g_b[:, :, 0:1]
    a_f = jnp.max(tot_f - g_f + x[:, :hh], axis=2, keepdims=True)
    a_b = jnp.max(tot_b - g_b + x[:, 2 * hh:3 * hh], axis=2, keepdims=True)
    m = jnp.zeros((hh, 1), F32)
    for c in range(nc):
        m_ref[0, c, 0:hh] = jnp.broadcast_to(m, (hh, l))
        m = jnp.maximum(tot_f[c] + m, a_f[c])
        m_ref[0, c, hh:2 * hh] = jnp.broadcast_to(m, (hh, l))
    m = jnp.zeros((hh, 1), F32)
    for c in reversed(range(nc)):
        m_ref[0, c, 2 * hh:3 * hh] = jnp.broadcast_to(m, (hh, l))
        m = jnp.maximum(tot_b[c] + m, a_b[c])
        m_ref[0, c, 3 * hh:] = jnp.broadcast_to(m, (hh, l))


def _gate_prep(graw, bias):
    b, nc, r, l = graw.shape
    spec = pl.BlockSpec((1, nc, r, l), lambda i: (i, 0, 0, 0))
    shape = jax.ShapeDtypeStruct(graw.shape, F32)
    return pl.pallas_call(
        _gates_kernel,
        grid=(b,),
        in_specs=[spec, pl.BlockSpec((r, 1), lambda i: (0, 0))],
        out_specs=[spec] * 4,
        out_shape=[shape] * 4,
        compiler_params=_params(1),
        name="mlstm_gate_prep",
    )(graw, bias.reshape(r, 1))


def _row_to_col(row_vec, eye):
    return jnp.sum(jnp.where(eye, row_vec, 0.0), axis=1, keepdims=True)


def _mlstm_local(q, kt, ig, g, tot, m, m_new, *, causal):
    l = q.shape[0]
    r_i = lax.broadcasted_iota(jnp.int32, (l, l), 0)
    c_i = lax.broadcasted_iota(jnp.int32, (l, l), 1)
    g_col = _row_to_col(g, r_i == c_i)
    d = g_col - (g - ig)
    d = jnp.where((c_i <= r_i) if causal else (c_i >= r_i), d, -jnp.inf)
    gm = g_col + m
    m_t = jnp.maximum(gm, jnp.max(d, axis=1, keepdims=True))
    s_ = jnp.dot(q, kt, preferred_element_type=F32) * jnp.exp(d - m_t)
    w = jnp.exp(tot - g + ig - m_new)
    return dict(
        s=s_.astype(BF16), s_sum=jnp.sum(s_, axis=1, keepdims=True),
        inter=jnp.exp(gm - m_t), floor=jnp.exp(-m_t), decay=jnp.exp(tot + m - m_new),
        wkt=(kt.astype(F32) * w).astype(BF16))


def _mlstm_state(q, v, loc, c, n):
    l = q.shape[0]
    qc = jnp.dot(q, c.astype(BF16), preferred_element_type=F32)
    num = jnp.dot(loc["s"], v, preferred_element_type=F32) + loc["inter"] * qc
    qn = jnp.sum(q.astype(F32) * n, axis=1, keepdims=True)
    den = loc["s_sum"] + loc["inter"] * qn
    h = num * (1.0 / jnp.maximum(jnp.abs(den), loc["floor"]))
    c_new = loc["decay"] * c + jnp.dot(loc["wkt"], v, preferred_element_type=F32)
    ones = jnp.ones((BF16_SUBLANES, l), BF16)
    n_new = loc["decay"] * n + lax.dot_general(ones, loc["wkt"], _dot_dims(True), preferred_element_type=F32)[0:1]
    return h, c_new, n_new


MLSTM_UNROLL = 2


def _mlstm_kernel(q_ref, kt_ref, v_ref, og_ref, gp_ref, ng_ref, o_ref, acc_ref, cf_ref, cb_ref):
    s, dv = v_ref.shape[1:]
    dk = q_ref.shape[2]
    l = MLSTM_CHUNK
    nc = s // l
    cf_ref[...] = jnp.zeros_like(cf_ref)
    cb_ref[...] = jnp.zeros_like(cb_ref)
    gain = ng_ref[0]

    def rows_of(c):
        return pl.ds(pl.multiple_of(c * l, l), l)

    def local(c, causal):
        gp = gp_ref[0, 0, c]
        if causal:
            ig, g, m, m_new = gp[0:1], gp[1:2], gp[4:5, 0:1], gp[5:6, 0:1]
            tot = g[:, l - 1:l]
        else:
            ig, g, m, m_new = gp[2:3], gp[3:4], gp[6:7, 0:1], gp[7:8, 0:1]
            tot = g[:, 0:1]
        return _mlstm_local(q_ref[0, rows_of(c), :], kt_ref[0, c], ig, g, tot, m, m_new, causal=causal)

    def emit(rows, h, first_touch):
        if first_touch:
            acc_ref[rows, :] = h
        else:
            y = _rms(acc_ref[rows, :] + h) * gain * og_ref[0, rows, :].astype(F32)
            o_ref[0, rows, :] = y.astype(o_ref.dtype)

    def body(first_touch, i, carry):
        chunks = [(i * MLSTM_UNROLL + u, nc - 1 - (i * MLSTM_UNROLL + u)) for u in range(MLSTM_UNROLL)]
        n_f, n_b = carry
        c_f, c_b = cf_ref[...], cb_ref[...]
        for cf, cb in chunks:
            rows = rows_of(cf)
            h, c_f, n_f = _mlstm_state(q_ref[0, rows, :], v_ref[0, rows, :], local(cf, True), c_f, n_f)
            emit(rows, h, first_touch)
            rows = rows_of(cb)
            h, c_b, n_b = _mlstm_state(q_ref[0, rows, :], v_ref[0, rows, :], local(cb, False), c_b, n_b)
            emit(rows, h, first_touch)
        cf_ref[...] = c_f
        cb_ref[...] = c_b
        return n_f, n_b

    n0 = jnp.zeros((1, dk), F32)
    n_it = nc // MLSTM_UNROLL
    carry = lax.fori_loop(0, n_it // 2, functools.partial(body, True), (n0, n0))
    lax.fori_loop(n_it // 2, n_it, functools.partial(body, False), carry)


def _mlstm(q, kt, qkv, og, gate_pack, norm_g, heads, dk, dv, v_col0):
    b, s, _ = qkv.shape
    assert s % (2 * MLSTM_UNROLL * MLSTM_CHUNK) == 0 and v_col0 % dv == 0
    nc = s // MLSTM_CHUNK
    seq_dk = pl.BlockSpec((1, s, dk), lambda i, h: (i, 0, h))
    seq_dv = pl.BlockSpec((1, s, dv), lambda i, h: (i, 0, h))
    return pl.pallas_call(
        _mlstm_kernel,
        grid=(b, heads),
        in_specs=[seq_dk,
                  pl.BlockSpec((1, nc, dk, MLSTM_CHUNK), lambda i, h: (i, 0, h, 0)),
                  pl.BlockSpec((1, s, dv), lambda i, h: (i, 0, v_col0 // dv + h)),
                  seq_dv,
                  pl.BlockSpec((1, 1, nc, 8, MLSTM_CHUNK), lambda i, h: (i, h, 0, 0, 0)),
                  pl.BlockSpec((1, 1, dv), lambda i, h: (h, 0, 0))],
        out_specs=seq_dv,
        out_shape=jax.ShapeDtypeStruct((b, s, heads * dv), BF16),
        scratch_shapes=[pltpu.VMEM((s, dv), F32),
                        pltpu.VMEM((dk, dv), F32),
                        pltpu.VMEM((dk, dv), F32)],
        compiler_params=_params(2),
        name="mlstm_bidirectional",
    )(q, kt, qkv, og, gate_pack, norm_g.reshape(heads, 1, dv))


def _rpb_toeplitz_kernel(r_ref, o_ref, *, win_c):
    dpad, n = r_ref.shape[1], o_ref.shape[1]
    dd = lax.broadcasted_iota(jnp.int32, (dpad, n), 0)
    cc = lax.broadcasted_iota(jnp.int32, (dpad, n), 1)
    shift = GRID_W.bit_length() - 1
    qc, kc = cc >> shift, cc & (GRID_W - 1)
    dc = jnp.clip(kc - qc + (win_c - 1), 0, 2 * win_c - 2)
    onehot = jnp.where(dd == dc, 1.0, 0.0).astype(F32)
    vals = jnp.dot(r_ref[...], onehot, precision=HIGHEST, preferred_element_type=F32) * LOG2E
    co = lax.broadcasted_iota(jnp.int32, (1, n), 1)
    qo, ko = co >> shift, co & (GRID_W - 1)
    cs = jnp.clip(qo - win_c // 2, 0, GRID_W - win_c)
    o_ref[...] = jnp.where((ko >= cs) & (ko < cs + win_c), vals, -jnp.inf)


def _natten_plan(n_rows, win_r, q_rows, k_rows):
    variants, var_idx, k_start = [], [], []
    for r0 in range(0, n_rows, q_rows):
        kp = min(max(r0 - win_r // 2, 0), n_rows - k_rows)
        pattern = []
        for i in range(q_rows):
            r = r0 + i
            rs = min(max(r - win_r // 2, 0), n_rows - win_r)
            assert kp <= rs and rs + win_r <= kp + k_rows
            pattern.append(tuple((kp + jj - r + win_r - 1) if rs <= kp + jj < rs + win_r else None
                                 for jj in range(k_rows)))
        pattern = tuple(pattern)
        if pattern not in variants:
            variants.append(pattern)
        var_idx.append(variants.index(pattern))
        k_start.append(kp)
    return variants, var_idx, k_start


def _bias_assemble_kernel(t_ref, o_ref, *, variants):
    w = GRID_W
    masked = jnp.full((w, w), -jnp.inf, F32)
    for v, pattern in enumerate(variants):
        for i, row in enumerate(pattern):
            tiles = [masked if dr is None else t_ref[0, dr] for dr in row]
            o_ref[0, v, i * w:(i + 1) * w, :] = jnp.concatenate(tiles, axis=1)


def _natten_bias(rpb, variants):
    hb, n_dr, n_dc = rpb.shape
    win_c = (n_dc + 1) // 2
    w = GRID_W
    dpad = 32
    assert n_dc <= dpad and (hb * n_dr) % 8 == 0
    r = jnp.pad(rpb, ((0, 0), (0, 0), (0, dpad - n_dc))).reshape(hb * n_dr, dpad)
    toeplitz = pl.pallas_call(
        functools.partial(_rpb_toeplitz_kernel, win_c=win_c),
        grid=(1,),
        in_specs=[pl.BlockSpec(r.shape, lambda i: (0, 0))],
        out_specs=pl.BlockSpec((r.shape[0], w * w), lambda i: (0, 0)),
        out_shape=jax.ShapeDtypeStruct((r.shape[0], w * w), F32),
        compiler_params=_params(1),
        name="natten_rpb_toeplitz",
    )(r).reshape(hb, n_dr, w, w)
    q_rows, k_rows = len(variants[0]), len(variants[0][0])
    out_blk = (1, len(variants), q_rows * w, k_rows * w)
    return pl.pallas_call(
        functools.partial(_bias_assemble_kernel, variants=variants),
        grid=(hb,),
        in_specs=[pl.BlockSpec((1, n_dr, w, w), lambda h: (h, 0, 0, 0))],
        out_specs=pl.BlockSpec(out_blk, lambda h: (h, 0, 0, 0)),
        out_shape=jax.ShapeDtypeStruct((hb,) + out_blk[1:], F32),
        compiler_params=_params(1),
        name="natten_bias_assemble",
    )(toeplitz)


NATTEN_Q_ROWS = 1
NATTEN_BATCH = 16


def _natten_kernel(var_ref, ks_ref, q_ref, k_ref, v_ref, z_ref, bias_ref, qg_ref, kg_ref, o_ref,
                   qn_ref, kn_ref, sc_a, sc_b):
    s, hd = q_ref.shape[1:]
    w = GRID_W
    nq, nk = bias_ref.shape[2:]
    n_batches = s // (nq * NATTEN_BATCH)
    def unit_rows(x):
        return x * lax.rsqrt(jnp.sum(x * x, axis=-1, keepdims=True) + hd * EPS)

    qn_ref[...] = (unit_rows(q_ref[0].astype(F32)) * (qg_ref[...] * LOG2E)).astype(BF16)
    kn_ref[...] = (unit_rows(k_ref[0].astype(F32)) * (kg_ref[...] * (float(hd) ** 0.5))).astype(BF16)

    def rows_of(p):
        return (pl.ds(pl.multiple_of(p * nq, nq), nq), pl.ds(pl.multiple_of(ks_ref[p] * w, w), nk))

    def scores(p):
        q_rows, kv_rows = rows_of(p)
        qk = lax.dot_general(qn_ref[q_rows, :], kn_ref[kv_rows, :], (((1,), (1,)), ((), ())),
                             preferred_element_type=F32)
        return qk + bias_ref[0, var_ref[p]]

    def attend(p, sc):
        q_rows, kv_rows = rows_of(p)
        p_un = jnp.exp2(sc - jnp.max(sc, axis=1, keepdims=True))
        o = jnp.dot(p_un.astype(BF16), v_ref[0, kv_rows, :], preferred_element_type=F32)
        o = o * (1.0 / jnp.sum(p_un, axis=1, keepdims=True))
        o_ref[0, q_rows, :] = (o * _silu(z_ref[0, q_rows, :].astype(F32))).astype(o_ref.dtype)

    def fill(buf, j):
        for u in range(NATTEN_BATCH):
            buf[u] = scores(j * NATTEN_BATCH + u)

    def drain(buf, j):
        for u in range(NATTEN_BATCH):
            attend(j * NATTEN_BATCH + u, buf[u])

    def body(k, carry):
        fill(sc_b, 2 * k + 1)
        drain(sc_a, 2 * k)
        fill(sc_a, 2 * k + 2)
        drain(sc_b, 2 * k + 1)
        return carry

    fill(sc_a, 0)
    lax.fori_loop(0, n_batches // 2 - 1, body, 0)
    fill(sc_b, n_batches - 1)
    drain(sc_a, n_batches - 2)
    drain(sc_b, n_batches - 1)


def _natten(proj, rpb, q_gain, k_gain, heads, hd, col0):
    b, s, _ = proj.shape
    win_r = (rpb.shape[1] + 1) // 2
    n_rows = s // GRID_W
    k_rows = win_r + NATTEN_Q_ROWS - 1
    k_rows += k_rows % 2
    assert s % GRID_W == 0 and n_rows >= k_rows and col0 % hd == 0
    assert n_rows % (2 * NATTEN_Q_ROWS * NATTEN_BATCH) == 0
    variants, var_idx, k_start = _natten_plan(n_rows, win_r, NATTEN_Q_ROWS, k_rows)
    bias = _natten_bias(rpb, variants)
    sc_buf = pltpu.VMEM((NATTEN_BATCH,) + bias.shape[2:], F32)
    c0 = col0 // hd

    def col(part):
        return pl.BlockSpec((1, s, hd), lambda h, i: (i, 0, c0 + part * heads + h))

    smem = pl.BlockSpec(memory_space=pltpu.SMEM)
    vec = pl.BlockSpec((1, hd), lambda h, i: (0, 0))
    return pl.pallas_call(
        _natten_kernel,
        grid=(heads, b),
        in_specs=[smem, smem, col(0), col(1), col(2), col(3),
                  pl.BlockSpec((1,) + bias.shape[1:], lambda h, i: (h, 0, 0, 0)), vec, vec],
        out_specs=pl.BlockSpec((1, s, hd), lambda h, i: (i, 0, h)),
        out_shape=jax.ShapeDtypeStruct((b, s, heads * hd), BF16),
        scratch_shapes=[pltpu.VMEM((s, hd), BF16), pltpu.VMEM((s, hd), BF16), sc_buf, sc_buf],
        compiler_params=_params(2),
        name="neighbourhood_attention",
    )(jnp.asarray(var_idx, jnp.int32), jnp.asarray(k_start, jnp.int32), proj, proj, proj, proj, bias,
      q_gain.reshape(1, hd), k_gain.reshape(1, hd))


def _mm_out_head_kernel(a1_ref, a2_ref, w1_ref, w2_ref, x_ref, gate_ref, o_ref, wb1_ref, wb2_ref):
    wb1, wb2 = w1_ref[...].astype(BF16), w2_ref[...].astype(BF16)
    wb1_ref[...] = wb1
    wb2_ref[...] = wb2
    acc = jnp.dot(a1_ref[0], wb1, preferred_element_type=F32)
    acc = acc + jnp.dot(a2_ref[0], wb2, preferred_element_type=F32)
    o_ref[0] = x_ref[0] + gate_ref[0] * acc


def _mm_out_kernel(a1_ref, a2_ref, w1_ref, w2_ref, x_ref, gate_ref, _, o_ref):
    acc = jnp.dot(a1_ref[0], w1_ref[...], preferred_element_type=F32)
    acc = acc + jnp.dot(a2_ref[0], w2_ref[...], preferred_element_type=F32)
    o_ref[0] = x_ref[0] + gate_ref[0] * acc


def _out_proj(a1, a1_blk, a2, a2_blk, w, x, gate, name):
    b, s, d = x.shape
    kh = w.shape[0] // 2
    tm, tn = _tile(s, 512), _tile(d, 512)
    tn_head = _tile(tn, 256)
    mt = s // tm
    gate3 = gate.reshape(b, 1, d)
    out_shape = jax.ShapeDtypeStruct((b, s, d), F32)
    wb_shape = jax.ShapeDtypeStruct((kh, d), BF16)
    out, wb1, wb2 = pl.pallas_call(
        _mm_out_head_kernel,
        grid=(d // tn_head,),
        in_specs=[pl.BlockSpec((1, tm, kh), lambda n: (0, 0, a1_blk)),
                  pl.BlockSpec((1, tm, kh), lambda n: (0, 0, a2_blk)),
                  pl.BlockSpec((kh, tn_head), lambda n: (0, n)),
                  pl.BlockSpec((kh, tn_head), lambda n: (1, n)),
                  pl.BlockSpec((1, tm, tn_head), lambda n: (0, 0, n)),
                  pl.BlockSpec((1, 1, tn_head), lambda n: (0, 0, n))],
        out_specs=[pl.BlockSpec((1, tm, tn_head), lambda n: (0, 0, n)),
                   pl.BlockSpec((kh, tn_head), lambda n: (0, n)),
                   pl.BlockSpec((kh, tn_head), lambda n: (0, n))],
        out_shape=[out_shape, wb_shape, wb_shape],
        compiler_params=_params(1),
        name=name + "_head",
    )(a1, a2, w, w, x, gate3)
    if b * mt == 1:
        return out

    def bm(t):
        return (t + 1) // mt, (t + 1) % mt

    return pl.pallas_call(
        _mm_out_kernel,
        grid=(b * mt - 1, d // tn),
        in_specs=[pl.BlockSpec((1, tm, kh), lambda t, n: (*bm(t), a1_blk)),
                  pl.BlockSpec((1, tm, kh), lambda t, n: (*bm(t), a2_blk)),
                  pl.BlockSpec((kh, tn), lambda t, n: (0, n)),
                  pl.BlockSpec((kh, tn), lambda t, n: (0, n)),
                  pl.BlockSpec((1, tm, tn), lambda t, n: (*bm(t), n)),
                  pl.BlockSpec((1, 1, tn), lambda t, n: (bm(t)[0], 0, n)),
                  pl.BlockSpec(memory_space=pl.ANY)],
        out_specs=pl.BlockSpec((1, tm, tn), lambda t, n: (*bm(t), n)),
        out_shape=out_shape,
        input_output_aliases={6: 0},
        compiler_params=_params(2),
        name=name,
    )(a1, a2, wb1, wb2, x, gate3, out)


def _spatial_kernel(gv_ref, uz_ref, ws_ref, bs_ref, vg_ref, o_ref):
    groups, l = ws_ref.shape[:2]
    cg = gv_ref.shape[2] // groups
    for r0 in range(0, gv_ref.shape[1], l):
        rows = slice(r0, r0 + l)
        vn = (_rms(gv_ref[0, rows, :].astype(F32)) * vg_ref[...]).astype(BF16)
        for g in range(groups):
            cols = slice(g * cg, (g + 1) * cg)
            sv = jnp.dot(ws_ref[g], vn[:, cols], preferred_element_type=F32) + bs_ref[:, g:g + 1]
            o_ref[0, rows, cols] = (uz_ref[0, rows, cols].astype(F32) * sv).astype(BF16)


SPATIAL_CHUNKS_PER_STEP = 2


def _spatial(gv, uz, ws, bs_t, v_gain):
    b, s, cw = gv.shape
    l = SPATIAL_CHUNK
    groups = ws.shape[0]
    assert s % l == 0 and ws.shape[1:] == (l, l)
    rows = _tile(s, l * SPATIAL_CHUNKS_PER_STEP)
    blk = pl.BlockSpec((1, rows, cw), lambda i, n: (i, n, 0))
    return pl.pallas_call(
        _spatial_kernel,
        grid=(b, s // rows),
        in_specs=[blk, blk,
                  pl.BlockSpec(ws.shape, lambda i, n: (0, 0, 0)),
                  pl.BlockSpec((l, groups), lambda i, n: (0, 0)),
                  pl.BlockSpec((1, cw), lambda i, n: (0, 0))],
        out_specs=blk,
        out_shape=jax.ShapeDtypeStruct((b, s, cw), BF16),
        compiler_params=_params(2),
        name="gmlp_spatial_gate",
    )(gv, uz, ws, bs_t, v_gain.reshape(1, cw))


def kernel(x, c, norm_g0, ada_w0, ada_b0, w_in0, a_conv_w, a_gate_b, a_norm_g, b_q_gain, b_k_gain, b_rpb, w_out0,
           norm_g1, ada_w1, ada_b1, w_in1, c_v_norm_g, c_w_s, c_b_s, w_out1):
    b, s, d = x.shape
    a_heads, dv = a_norm_g.shape
    a_width = a_heads * dv
    dk = dv // 2
    qk_width = a_heads * dk
    hd = b_q_gain.shape[0]
    b_heads, b_width = b_rpb.shape[0], b_rpb.shape[0] * hd
    n_gate = 4 * a_heads
    a_cols = 2 * qk_width + 3 * a_width
    assert w_in0.shape[1] == a_cols + n_gate + 4 * b_width
    cw = c_v_norm_g.shape[0]
    nc = s // MLSTM_CHUNK

    c_pad = jnp.pad(c, ((0, -b % 8), (0, 0)))
    mod0 = _modulation(c_pad, ada_w0, ada_b0)[:b]
    mod1 = _modulation(c_pad, ada_w1, ada_b1)[:b]
    shift0, scale0, gate0 = mod0[:, :d], mod0[:, d:2 * d], mod0[:, 2 * d:]
    shift1, scale1, gate1 = mod1[:, :d], mod1[:, d:2 * d], mod1[:, 2 * d:]

    qkv_cols = 2 * qk_width + a_width
    wt0 = w_in0.T
    assert a_cols % LANES == 0 and n_gate <= LANES
    h0, gates = _norm_mod(x, norm_g0, scale0, shift0, wt0, a_cols)
    h0, gates = h0.reshape(b * s, d), gates.reshape(b * s, LANES)[:, :n_gate]
    qkv = _proj(h0, wt0, [0], qkv_cols, _tile(qkv_cols, 1024), _identity, BF16, "layer0_in_proj_qkv",
                w_transposed=True).reshape(b, s, qkv_cols)
    og = _proj(h0, wt0, [qkv_cols, qkv_cols + a_width], a_width, _tile(a_width, 512), _og_gate, BF16,
               "layer0_in_proj_og", w_transposed=True).reshape(b, s, a_width)
    proj_b = _proj(h0, wt0, [a_cols + n_gate], 4 * b_width, _tile(4 * b_width, 1024), _identity, BF16,
                   "layer0_in_proj_b", w_transposed=True).reshape(b, s, 4 * b_width)

    q, kt = _conv_silu(qkv, a_conv_w, qk_width, dk)
    graw = gates.reshape(b, nc, MLSTM_CHUNK, n_gate).transpose(0, 1, 3, 2)
    ig, pre, suf, mst = _gate_prep(graw, a_gate_b)
    hh = a_heads
    pack = jnp.stack([ig[:, :, :hh], pre[:, :, hh:2 * hh], ig[:, :, 2 * hh:3 * hh], suf[:, :, 3 * hh:],
                      mst[:, :, :hh], mst[:, :, hh:2 * hh], mst[:, :, 2 * hh:3 * hh], mst[:, :, 3 * hh:]],
                     axis=3)
    pack = pack.transpose(0, 2, 1, 3, 4)
    y_a = _mlstm(q, kt, qkv, og, pack, a_norm_g, a_heads, dk, dv, 2 * qk_width)

    y_b = _natten(proj_b, b_rpb, b_q_gain, b_k_gain, b_heads, hd, 0)

    assert a_width == b_width
    x1 = _out_proj(y_a, 0, y_b, 0, w_out0, x, gate0, "layer0_out_proj")

    h1 = _norm_mod(x1, norm_g1, scale1, shift1).reshape(b * s, d)
    uz = _proj(h1, w_in1, [0, 2 * cw], cw, _tile(cw, 512), _uz_gate, BF16, "gmlp_in_proj_uz")
    gv = _proj(h1, w_in1, [cw], cw, _tile(cw, 1024), _gelu_tanh, BF16, "gmlp_in_proj_v")
    y = _spatial(gv.reshape(b, s, cw), uz.reshape(b, s, cw), c_w_s.astype(BF16), c_b_s.T, c_v_norm_g)
    return _out_proj(y, 0, y, 1, w_out1, x1, gate1, "layer1_out_proj")
```
